```python
import math
import jax, jax.numpy as jnp
from jax import lax
import numpy as np

D_MODEL = 2048
BATCH = 4
SEQ = 4096
DEPTH = 1

HEAD_DIM = 128
D_MIX = D_MODEL
N_HEADS_DSWA = D_MIX // (2 * HEAD_DIM)
N_HEADS_SB = D_MIX // HEAD_DIM - N_HEADS_DSWA
W_DSWA = N_HEADS_DSWA * HEAD_DIM
W_SB = N_HEADS_SB * HEAD_DIM
DSWA_PATTERNS = ((128, 1), (512, 4), (2048, 16))
REL_BUCKETS = 32
REL_MAX_DISTANCE = 2048
SB_BLOCK = 128
N_EXPERTS = 64
TOP_K = 6
EXPERT_DIM = 1408
SHARED_DIM = 1408
ROUTED_SCALING = 2.446
MOE_ROW_BLOCK = 128
NORM_EPS = 1e-6
N_MOD = 6

kernel_name = "hybrid_dilated_stickbreaking_moe_block"


def rmsnorm(x, g):
    xf = x.astype(jnp.float32)
    y = xf * lax.rsqrt(jnp.mean(xf * xf, axis=-1, keepdims=True) + NORM_EPS)
    return (y * g.astype(jnp.float32)).astype(x.dtype)


def head_rmsnorm(o, g):
    H, E = o.shape[2], o.shape[3]
    return rmsnorm(o, g.reshape(H, E)).reshape(o.shape[0], o.shape[1], H * E)


def t5_causal_bucket(dist):
    max_exact = REL_BUCKETS // 2
    is_small = dist < max_exact
    d_f = jnp.maximum(dist, max_exact).astype(jnp.float32)
    large = max_exact + (jnp.log(d_f / max_exact) / math.log(REL_MAX_DISTANCE / max_exact)
                         * (REL_BUCKETS - max_exact)).astype(jnp.int32)
    large = jnp.minimum(large, REL_BUCKETS - 1)
    return jnp.where(is_small, dist, large)


def dilated_window_branch(q, k, v, rel_bias, window, dilation):
    B, S, H, E = q.shape
    L = S // dilation
    steps = window // dilation
    blk = steps
    nb = -(-L // blk)
    Lp = nb * blk

    def phases(t):
        t = t.reshape(B, L, dilation, H, E).transpose(0, 2, 3, 1, 4)
        return jnp.pad(t, ((0, 0), (0, 0), (0, 0), (0, Lp - L), (0, 0)))

    def band(t):
        tp = jnp.pad(t, ((0, 0), (0, 0), (0, 0), (blk, 0), (0, 0)))
        prev = tp[:, :, :, :Lp].reshape(B, dilation, H, nb, blk, E)
        cur = tp[:, :, :, blk:].reshape(B, dilation, H, nb, blk, E)
        return jnp.concatenate([prev, cur], axis=4)

    qb = phases(q).reshape(B, dilation, H, nb, blk, E)
    kb = band(phases(k))
    vb = band(phases(v))

    i = jnp.arange(blk)[:, None]
    j = jnp.arange(2 * blk)[None, :]
    diff = i + blk - j
    key_step = jnp.arange(nb)[:, None, None] * blk - blk + j
    mask = (diff >= 0) & (diff <= steps) & (key_step >= 0)
    bucket = t5_causal_bucket(jnp.maximum(diff, 0) * dilation)
    bias = rel_bias[bucket].astype(jnp.float32).transpose(2, 0, 1)[:, None]

    s = jnp.einsum('brhnqe,brhnke->brhnqk', qb, kb).astype(jnp.float32) / math.sqrt(E) + bias
    s = jnp.where(mask, s, -jnp.inf)
    lse = jax.nn.logsumexp(s, axis=-1)
    p = jnp.exp(s - lse[..., None])
    o = jnp.einsum('brhnqk,brhnke->brhnqe', p.astype(v.dtype), vb)
    o = o.reshape(B, dilation, H, Lp, E)[:, :, :, :L].transpose(0, 3, 1, 2, 4).reshape(B, S, H, E)
    lse = lse.reshape(B, dilation, H, Lp)[..., :L].transpose(0, 3, 1, 2).reshape(B, S, H)
    return o, lse


def dilated_attention(q, k, v, rel_bias):
    outs, lses = [], []
    for window, dilation in DSWA_PATTERNS:
        o, l = dilated_window_branch(q, k, v, rel_bias, window, dilation)
        outs.append(o)
        lses.append(l)
    w = jax.nn.softmax(jnp.stack(lses, axis=0), axis=0)
    o = jnp.stack(outs, axis=0).astype(jnp.float32)
    return jnp.sum(w[..., None] * o, axis=0).astype(q.dtype)


def stick_breaking_attention(q, k, v):
    B, S, H, E = q.shape
    nq = S // SB_BLOCK
    kh = k.transpose(0, 2, 1, 3)
    vh = v.transpose(0, 2, 1, 3)
    qblocks = q.transpose(0, 2, 1, 3).reshape(B, H, nq, SB_BLOCK, E).transpose(2, 0, 1, 3, 4)
    key_pos = jnp.arange(S)

    def one_block(args):
        qblk, blk_idx = args
        z = jnp.einsum('bhqe,bhke->bhqk', qblk, kh).astype(jnp.float32) / math.sqrt(E)
        q_pos = blk_idx * SB_BLOCK + jnp.arange(SB_BLOCK)
        mask = key_pos[None, :] < q_pos[:, None]
        log_keep = jnp.where(mask, jax.nn.log_sigmoid(-z), 0.0)
        log_after = lax.cumsum(log_keep, axis=3, reverse=True) - log_keep
        a = jnp.where(mask, jnp.exp(jax.nn.log_sigmoid(z) + log_after), 0.0)
        return jnp.einsum('bhqk,bhke->bhqe', a.astype(vh.dtype), vh)

    out = lax.map(one_block, (qblocks, jnp.arange(nq)))
    return out.transpose(1, 0, 3, 2, 4).reshape(B, S, H, E)


def swiglu(x, w_gate, w_up, w_down):
    return (jax.nn.silu(x @ w_gate) * (x @ w_up)) @ w_down


def routed_experts(xf, idx, gates, w_gate, w_up, w_down):
    n_tok, d = xf.shape
    n_rows = n_tok * TOP_K
    n_pad = -(-(n_rows + N_EXPERTS * (MOE_ROW_BLOCK - 1)) // MOE_ROW_BLOCK) * MOE_ROW_BLOCK
    n_blocks = n_pad // MOE_ROW_BLOCK
    e_flat = idx.reshape(-1)
    tok_flat = jnp.arange(n_rows, dtype=jnp.int32) // TOP_K
    g_flat = gates.reshape(-1)
    order = jnp.argsort(e_flat, stable=True)
    e_sorted = e_flat[order]
    counts = jnp.bincount(e_flat, length=N_EXPERTS)
    padded = (counts + MOE_ROW_BLOCK - 1) // MOE_ROW_BLOCK * MOE_ROW_BLOCK
    pad_end = jnp.cumsum(padded)
    pad_start = pad_end - padded
    grp_start = jnp.cumsum(counts) - counts
    dest = pad_start[e_sorted] + (jnp.arange(n_rows) - grp_start[e_sorted])
    row_tok = jnp.zeros((n_pad,), jnp.int32).at[dest].set(tok_flat[order])
    row_gate = jnp.zeros((n_pad,), gates.dtype).at[dest].set(g_flat[order])
    blk_expert = jnp.minimum(
        jnp.searchsorted(pad_end, jnp.arange(n_blocks) * MOE_ROW_BLOCK, side='right'), N_EXPERTS - 1)

    def expert_block(args):
        tok_b, e, g_b = args
        xb = xf[tok_b]
        return swiglu(xb, w_gate[e], w_up[e], w_down[e]) * g_b[:, None]

    ys = lax.map(expert_block, (row_tok.reshape(n_blocks, MOE_ROW_BLOCK), blk_expert,
                                row_gate.reshape(n_blocks, MOE_ROW_BLOCK)))
    return jax.ops.segment_sum(ys.reshape(n_pad, d), row_tok, num_segments=n_tok)


def moe_ffn(h, w_router, router_bias, w_gate_e, w_up_e, w_down_e, w_gate_s, w_up_s, w_down_s):
    B, S, D = h.shape
    hf = h.reshape(B * S, D)
    scores = jax.nn.sigmoid((hf @ w_router).astype(jnp.float32))
    _, idx = lax.top_k(scores + router_bias.astype(jnp.float32), TOP_K)
    g = jnp.take_along_axis(scores, idx, axis=-1)
    g = g / jnp.sum(g, axis=-1, keepdims=True) * ROUTED_SCALING
    routed = routed_experts(hf, idx, g.astype(h.dtype), w_gate_e, w_up_e, w_down_e)
    shared = swiglu(hf, w_gate_s, w_up_s, w_down_s)
    return (routed + shared).reshape(B, S, D)


def setup_inputs(seed: int = 0) -> dict:
    key = jax.random.key(seed)
    ks = jax.random.split(key, 24)
    f32 = jnp.float32
    nrm = lambda k, shape, scale: jax.random.normal(k, shape, f32) * scale
    gain = lambda k, shape: 1.0 + 0.1 * jax.random.normal(k, shape, f32)
    return {
        "x": nrm(ks[0], (BATCH, SEQ, D_MODEL), 1.0),
        "c": nrm(ks[1], (BATCH, D_MODEL), 1.0),
        "rel_bias": nrm(ks[2], (REL_BUCKETS, N_HEADS_DSWA), 0.5),
        "norm_pre_attn": gain(ks[3], (DEPTH, D_MODEL)),
        "norm_post_attn": gain(ks[4], (DEPTH, D_MODEL)),
        "norm_pre_ffn": gain(ks[5], (DEPTH, D_MODEL)),
        "norm_post_ffn": gain(ks[6], (DEPTH, D_MODEL)),
        "w_ada": nrm(ks[7], (DEPTH, D_MODEL, N_MOD * D_MODEL), D_MODEL ** -0.5),
        "b_ada": nrm(ks[8], (DEPTH, N_MOD * D_MODEL), 0.02),
        "w_in": nrm(ks[9], (DEPTH, D_MODEL, 3 * D_MIX), D_MODEL ** -0.5),
        "head_gain_dswa": gain(ks[10], (DEPTH, W_DSWA)),
        "head_gain_sb": gain(ks[11], (DEPTH, W_SB)),
        "w_out": nrm(ks[12], (DEPTH, D_MIX, D_MODEL), D_MIX ** -0.5),
        "w_router": nrm(ks[13], (DEPTH, D_MODEL, N_EXPERTS), D_MODEL ** -0.5),
        "router_bias": nrm(ks[14], (DEPTH, N_EXPERTS), 0.01),
        "w_gate_e": nrm(ks[15], (DEPTH, N_EXPERTS, D_MODEL, EXPERT_DIM), D_MODEL ** -0.5),
        "w_up_e": nrm(ks[16], (DEPTH, N_EXPERTS, D_MODEL, EXPERT_DIM), D_MODEL ** -0.5),
        "w_down_e": nrm(ks[17], (DEPTH, N_EXPERTS, EXPERT_DIM, D_MODEL), EXPERT_DIM ** -0.5),
        "w_gate_s": nrm(ks[18], (DEPTH, D_MODEL, SHARED_DIM), D_MODEL ** -0.5),
        "w_up_s": nrm(ks[19], (DEPTH, D_MODEL, SHARED_DIM), D_MODEL ** -0.5),
        "w_down_s": nrm(ks[20], (DEPTH, SHARED_DIM, D_MODEL), SHARED_DIM ** -0.5),
    }


def reference(x, c, rel_bias, norm_pre_attn, norm_post_attn, norm_pre_ffn, norm_post_ffn,
              w_ada, b_ada, w_in, head_gain_dswa, head_gain_sb, w_out, w_router, router_bias,
              w_gate_e, w_up_e, w_down_e, w_gate_s, w_up_s, w_down_s):
    B, S, _ = x.shape
    c_act = jax.nn.silu(c)
    for l in range(DEPTH):
        mod = c_act @ w_ada[l] + b_ada[l]
        shift1, scale1, gate1, shift2, scale2, gate2 = [m[:, None, :] for m in jnp.split(mod, N_MOD, axis=-1)]

        h = rmsnorm(x, norm_pre_attn[l]) * (1.0 + scale1) + shift1
        qkv = h @ w_in[l]
        qa, ka, va, qb, kb, vb = jnp.split(qkv, 6, axis=-1)
        hd = lambda t: t.reshape(B, S, -1, HEAD_DIM)
        oa = dilated_attention(hd(qa), hd(ka), hd(va), rel_bias)
        ob = stick_breaking_attention(hd(qb), hd(kb), hd(vb))
        o = jnp.concatenate([head_rmsnorm(oa, head_gain_dswa[l]),
                             head_rmsnorm(ob, head_gain_sb[l])], axis=-1) @ w_out[l]
        x = x + gate1 * rmsnorm(o, norm_post_attn[l])

        h2 = rmsnorm(x, norm_pre_ffn[l]) * (1.0 + scale2) + shift2
        y = moe_ffn(h2, w_router[l], router_bias[l], w_gate_e[l], w_up_e[l], w_down_e[l],
                    w_gate_s[l], w_up_s[l], w_down_s[l])
        x = x + gate2 * rmsnorm(y, norm_post_ffn[l])
    return x
```

```python
import functools
import math

import jax
import jax.numpy as jnp
from jax import lax
from jax.experimental import pallas as pl
from jax.experimental.pallas import tpu as pltpu

F32 = jnp.float32
BF16 = jnp.bfloat16

HEAD_DIM = 128
DSWA_PATTERNS = ((128, 1), (512, 4), (2048, 16))
DSWA_BLK = 128
REL_BUCKETS = 32
REL_MAX_DISTANCE = 2048
TOP_K = 6
ROUTED_SCALING = 2.446
NORM_EPS = 1e-6
NEG_BIG = -1e30

V7X_VMEM_LIMIT = 56 * 1024 * 1024
MOE_TILE = 256
SB_TILE = 256
ROUTE_TILE = 512
COMBINE_TILE = 128
DMA_RING = 16


def _cparams(sem, vmem=None):
    return pltpu.CompilerParams(dimension_semantics=sem, vmem_limit_bytes=vmem)


def _rms(x, g):
    return x * lax.rsqrt(jnp.mean(x * x, axis=-1, keepdims=True) + NORM_EPS) * g


def _ada_kernel(c_ref, w_ref, b_ref, o_ref):
    c = c_ref[...]
    ca = c / (1.0 + jnp.exp(-c))
    o_ref[...] = jnp.dot(ca.astype(BF16), w_ref[...].astype(BF16),
                         preferred_element_type=F32) + b_ref[...]


def _ada(c_pad, w_ada, b_ada):
    rows, d = c_pad.shape
    n = w_ada.shape[1]
    tn = min(n, 1536)
    return pl.pallas_call(
        _ada_kernel,
        out_shape=jax.ShapeDtypeStruct((rows, n), F32),
        grid=(n // tn,),
        in_specs=[pl.BlockSpec((rows, d), lambda j: (0, 0)),
                  pl.BlockSpec((d, tn), lambda j: (0, j)),
                  pl.BlockSpec((1, tn), lambda j: (0, j))],
        out_specs=pl.BlockSpec((rows, tn), lambda j: (0, j)),
        compiler_params=_cparams(("arbitrary",), V7X_VMEM_LIMIT),
        name="ada",
    )(c_pad, w_ada, b_ada)


def _qkv_kernel(x_ref, mod_ref, g_ref, w_ref, o_ref, h_ref):
    @pl.when(pl.program_id(1) == 0)
    def _():
        h = _rms(x_ref[...], g_ref[...]) * (1.0 + mod_ref[1:2, :]) + mod_ref[0:1, :]
        h_ref[...] = h.astype(BF16)

    o_ref[...] = jnp.dot(h_ref[...], w_ref[...], preferred_element_type=F32)


def _qkv(x2, mod, g, w_bf, seq):
    n, d = x2.shape
    nout = w_bf.shape[1]
    tm = min(512, seq)
    tn = min(1536, nout)
    per_b = seq // tm
    return pl.pallas_call(
        _qkv_kernel,
        out_shape=jax.ShapeDtypeStruct((n, nout), F32),
        grid=(n // tm, nout // tn),
        in_specs=[pl.BlockSpec((tm, d), lambda i, j: (i, 0)),
                  pl.BlockSpec((None, 6, d), lambda i, j: (i // per_b, 0, 0)),
                  pl.BlockSpec((1, d), lambda i, j: (0, 0)),
                  pl.BlockSpec((d, tn), lambda i, j: (0, j))],
        out_specs=pl.BlockSpec((tm, tn), lambda i, j: (i, j)),
        scratch_shapes=[pltpu.VMEM((tm, d), BF16)],
        compiler_params=_cparams(("parallel", "arbitrary"), V7X_VMEM_LIMIT),
        name="qkv",
    )(x2, mod, g, w_bf)


def _dswa_kernel(rb_ref, q_ref, k_ref, v_ref, bucket_ref, g_ref, o_ref, acc_ref, m_ref, l_ref):
    seq = q_ref.shape[0]
    head = pl.program_id(1)
    scale = 1.0 / math.sqrt(HEAD_DIM)
    blk = DSWA_BLK
    ii = lax.broadcasted_iota(jnp.int32, (blk, 2 * blk), 0)
    jj = lax.broadcasted_iota(jnp.int32, (blk, 2 * blk), 1)
    diff = ii + blk - jj
    band = (diff >= 0) & (diff <= blk)

    for p, (_, dil) in enumerate(DSWA_PATTERNS):
        length = seq // dil
        nb = length // blk
        bucket = bucket_ref[p]
        bias = jnp.zeros((blk, 2 * blk), F32)
        for b in range(REL_BUCKETS):
            bias = jnp.where(bucket == b, rb_ref[head, b], bias)
        bias = jnp.where(band, bias, NEG_BIG)

        def rows(start, size, dil=dil):
            if dil == 1:
                return pl.ds(start, size)
            return pl.ds(start, size, stride=dil)

        def phase(r, carry, p=p, dil=dil, nb=nb, bias=bias, rows=rows):
            for n in range(nb):
                rq = rows(n * blk * dil + r, blk)
                q = (q_ref[rq, :] * scale).astype(BF16)
                if n == 0:
                    rk = rq
                    bs = bias[:, blk:]
                else:
                    rk = rows((n - 1) * blk * dil + r, 2 * blk)
                    bs = bias
                k = k_ref[rk, :].astype(BF16)
                v = v_ref[rk, :].astype(BF16)
                s = lax.dot_general(q, k, (((1,), (1,)), ((), ())),
                                    preferred_element_type=F32) + bs
                m_blk = jnp.max(s, axis=-1, keepdims=True)
                if p == 0:
                    m_new = m_blk
                    pexp = jnp.exp(s - m_new)
                    l_new = jnp.sum(pexp, axis=-1, keepdims=True)
                    acc_new = jnp.dot(pexp.astype(BF16), v, preferred_element_type=F32)
                else:
                    m_old = m_ref[rq, :][:, 0:1]
                    l_old = l_ref[rq, :][:, 0:1]
                    m_new = jnp.maximum(m_old, m_blk)
                    alpha = jnp.exp(m_old - m_new)
                    pexp = jnp.exp(s - m_new)
                    l_new = alpha * l_old + jnp.sum(pexp, axis=-1, keepdims=True)
                    acc_new = alpha * acc_ref[rq, :] + jnp.dot(
                        pexp.astype(BF16), v, preferred_element_type=F32)
                acc_ref[rq, :] = acc_new
                m_ref[rq, :] = jnp.broadcast_to(m_new, (blk, HEAD_DIM))
                l_ref[rq, :] = jnp.broadcast_to(l_new, (blk, HEAD_DIM))
            return carry

        lax.fori_loop(0, dil, phase, 0)

    chunk = min(512, seq)

    def finish(c, carry):
        r0 = pl.multiple_of(c * chunk, chunk)
        o = acc_ref[pl.ds(r0, chunk), :] / l_ref[pl.ds(r0, chunk), :]
        o_ref[pl.ds(r0, chunk), :] = _rms(o, g_ref[...]).astype(o_ref.dtype)
        return carry

    lax.fori_loop(0, seq // chunk, finish, 0)


def _dswa(rel_bias_t, qkv3, buckets, gain, n_heads):
    bsz, seq, _ = qkv3.shape
    w = n_heads * HEAD_DIM
    return pl.pallas_call(
        _dswa_kernel,
        out_shape=jax.ShapeDtypeStruct((bsz, seq, w), BF16),
        grid=(bsz, n_heads),
        in_specs=[pl.BlockSpec(memory_space=pltpu.SMEM),
                  pl.BlockSpec((None, seq, HEAD_DIM), lambda b, h: (b, 0, h)),
                  pl.BlockSpec((None, seq, HEAD_DIM), lambda b, h: (b, 0, n_heads + h)),
                  pl.BlockSpec((None, seq, HEAD_DIM), lambda b, h: (b, 0, 2 * n_heads + h)),
                  pl.BlockSpec(buckets.shape, lambda b, h: (0, 0, 0)),
                  pl.BlockSpec((1, HEAD_DIM), lambda b, h: (0, h))],
        out_specs=pl.BlockSpec((None, seq, HEAD_DIM), lambda b, h: (b, 0, h)),
        scratch_shapes=[pltpu.VMEM((seq, HEAD_DIM), F32),
                        pltpu.VMEM((seq, HEAD_DIM), F32),
                        pltpu.VMEM((seq, HEAD_DIM), F32)],
        compiler_params=_cparams(("parallel", "parallel"), V7X_VMEM_LIMIT),
        name="dswa",
    )(rel_bias_t, qkv3, qkv3, qkv3, buckets, gain)


def _sb_kernel(q_ref, k_ref, v_ref, tri_ref, g_ref, o_ref, kb_ref, vb_ref):
    tq = q_ref.shape[0]
    iq = pl.program_id(2)

    @pl.when(iq == 0)
    def _():
        kb_ref[...] = k_ref[...].astype(BF16)
        vb_ref[...] = v_ref[...].astype(BF16)

    q = (q_ref[...] * (1.0 / math.sqrt(HEAD_DIM))).astype(BF16)
    ti = lax.broadcasted_iota(jnp.int32, (tq, tq), 0)
    si = lax.broadcasted_iota(jnp.int32, (tq, tq), 1)
    causal = si < ti

    def block(jk, carry, acc, masked):
        r0 = pl.multiple_of(jk * tq, tq)
        k = kb_ref[pl.ds(r0, tq), :]
        v = vb_ref[pl.ds(r0, tq), :]
        z = lax.dot_general(q, k, (((1,), (1,)), ((), ())), preferred_element_type=F32)
        lk = -(jnp.maximum(z, 0.0) + jnp.log(1.0 + jnp.exp(-jnp.abs(z))))
        if masked:
            lk = jnp.where(causal, lk, 0.0)
        hi = lk.astype(BF16)
        lo = (lk - hi.astype(F32)).astype(BF16)
        after = jnp.dot(jnp.concatenate([hi, lo], axis=1), tri_ref[...],
                        preferred_element_type=F32)
        a = jnp.exp(z + lk + after + carry)
        if masked:
            a = jnp.where(causal, a, 0.0)
        acc = acc + jnp.dot(a.astype(BF16), v, preferred_element_type=F32)
        carry = carry + jnp.sum(lk, axis=-1, keepdims=True)
        return carry, acc

    carry0 = jnp.zeros((tq, 1), F32)
    acc0 = jnp.zeros((tq, HEAD_DIM), F32)
    carry, acc = block(iq, carry0, acc0, True)

    def body(t, state):
        return block(iq - 1 - t, state[0], state[1], False)

    carry, acc = lax.fori_loop(0, iq, body, (carry, acc))
    o_ref[...] = _rms(acc, g_ref[...]).astype(o_ref.dtype)


def _sb(qkv3, tri2, gain, n_heads, col0):
    bsz, seq, _ = qkv3.shape
    tq = min(SB_TILE, seq)
    w = n_heads * HEAD_DIM
    return pl.pallas_call(
        _sb_kernel,
        out_shape=jax.ShapeDtypeStruct((bsz, seq, w), BF16),
        grid=(bsz, n_heads, seq // tq),
        in_specs=[pl.BlockSpec((None, tq, HEAD_DIM), lambda b, h, i: (b, i, col0 + h)),
                  pl.BlockSpec((None, seq, HEAD_DIM), lambda b, h, i: (b, 0, col0 + n_heads + h)),
                  pl.BlockSpec((None, seq, HEAD_DIM), lambda b, h, i: (b, 0, col0 + 2 * n_heads + h)),
                  pl.BlockSpec(tri2.shape, lambda b, h, i: (0, 0)),
                  pl.BlockSpec((1, HEAD_DIM), lambda b, h, i: (0, h))],
        out_specs=pl.BlockSpec((None, tq, HEAD_DIM), lambda b, h, i: (b, i, h)),
        scratch_shapes=[pltpu.VMEM((seq, HEAD_DIM), BF16),
                        pltpu.VMEM((seq, HEAD_DIM), BF16)],
        compiler_params=_cparams(("parallel", "parallel", "arbitrary"), V7X_VMEM_LIMIT),
        name="sb",
    )(qkv3, qkv3, qkv3, tri2, gain)


def _outproj_kernel(oa_ref, ob_ref, w_ref, x_ref, mod_ref, gpa_ref, gpf_ref, wr_hi_ref, wr_lo_ref,
                    x1_ref, h2_ref, h2b_ref, lg_ref):
    wa = oa_ref.shape[1]
    o = jnp.dot(oa_ref[...], w_ref[0:wa, :], preferred_element_type=F32)
    o = o + jnp.dot(ob_ref[...], w_ref[wa:, :], preferred_element_type=F32)
    x1 = x_ref[...] + mod_ref[2:3, :] * _rms(o, gpa_ref[...])
    x1_ref[...] = x1
    h2 = _rms(x1, gpf_ref[...]) * (1.0 + mod_ref[4:5, :]) + mod_ref[3:4, :]
    h2_ref[...] = h2
    hi = h2.astype(BF16)
    h2b_ref[...] = hi
    lo = (h2 - hi.astype(F32)).astype(BF16)
    lg = jnp.dot(hi, wr_hi_ref[...], preferred_element_type=F32)
    lg = lg + jnp.dot(lo, wr_hi_ref[...], preferred_element_type=F32)
    lg = lg + jnp.dot(hi, wr_lo_ref[...], preferred_element_type=F32)
    lg_ref[...] = lg


def _outproj(oa, ob, w_bf, x2, mod, gpa, gpf, wr_hi, wr_lo, seq):
    n, d = x2.shape
    wa = oa.shape[1]
    ne = wr_hi.shape[1]
    tm = min(256, seq)
    per_b = seq // tm
    const = lambda i: (0, 0)
    return pl.pallas_call(
        _outproj_kernel,
        out_shape=(jax.ShapeDtypeStruct((n, d), F32), jax.ShapeDtypeStruct((n, d), F32),
                   jax.ShapeDtypeStruct((n, d), BF16), jax.ShapeDtypeStruct((n, ne), F32)),
        grid=(n // tm,),
        in_specs=[pl.BlockSpec((tm, wa), lambda i: (i, 0)),
                  pl.BlockSpec((tm, wa), lambda i: (i, 0)),
                  pl.BlockSpec(w_bf.shape, const),
                  pl.BlockSpec((tm, d), lambda i: (i, 0)),
                  pl.BlockSpec((None, 6, d), lambda i: (i // per_b, 0, 0)),
                  pl.BlockSpec((1, d), const),
                  pl.BlockSpec((1, d), const),
                  pl.BlockSpec(wr_hi.shape, const),
                  pl.BlockSpec(wr_lo.shape, const)],
        out_specs=(pl.BlockSpec((tm, d), lambda i: (i, 0)),
                   pl.BlockSpec((tm, d), lambda i: (i, 0)),
                   pl.BlockSpec((tm, d), lambda i: (i, 0)),
                   pl.BlockSpec((tm, ne), lambda i: (i, 0))),
        compiler_params=_cparams(("parallel",), V7X_VMEM_LIMIT),
        name="outproj",
    )(oa, ob, w_bf, x2, mod, gpa, gpf, wr_hi, wr_lo)


def _route_kernel(lg_ref, rbias_ref, tri_ref, idx_ref, rank_ref, gate_ref, cnt_ref, carry_ref):
    tr, ne = lg_ref.shape

    @pl.when(pl.program_id(0) == 0)
    def _():
        carry_ref[...] = jnp.zeros_like(carry_ref)

    lg = lg_ref[...]
    scores = 1.0 / (1.0 + jnp.exp(-lg))
    sel = scores + rbias_ref[...]
    lane = lax.broadcasted_iota(jnp.int32, (tr, ne), 1).astype(F32)
    onehots, gates, idxs = [], [], []
    chosen = jnp.zeros((tr, ne), F32)
    for _ in range(TOP_K):
        m = jnp.max(sel, axis=-1, keepdims=True)
        idx = jnp.min(jnp.where(sel == m, lane, float(ne)), axis=-1, keepdims=True)
        oh = lane == idx
        gates.append(jnp.sum(jnp.where(oh, scores, 0.0), axis=-1, keepdims=True))
        sel = jnp.where(oh, -jnp.inf, sel)
        chosen = jnp.where(oh, 1.0, chosen)
        onehots.append(oh)
        idxs.append(idx)
    gsum = gates[0]
    for g in gates[1:]:
        gsum = gsum + g
    before = jnp.dot(tri_ref[...], chosen.astype(BF16), preferred_element_type=F32) + carry_ref[0:1, :]
    carry_ref[0:1, :] = carry_ref[0:1, :] + jnp.sum(chosen, axis=0, keepdims=True)

    out_lane = lax.broadcasted_iota(jnp.int32, (tr, 128), 1)
    idx_o = jnp.zeros((tr, 128), jnp.int32)
    rank_o = jnp.zeros((tr, 128), jnp.int32)
    gate_o = jnp.zeros((tr, 128), F32)
    for k in range(TOP_K):
        rank = jnp.sum(jnp.where(onehots[k], before, 0.0), axis=-1, keepdims=True)
        idx_o = jnp.where(out_lane == k, idxs[k].astype(jnp.int32), idx_o)
        rank_o = jnp.where(out_lane == k, rank.astype(jnp.int32), rank_o)
        gate_o = jnp.where(out_lane == k, gates[k] / gsum * ROUTED_SCALING, gate_o)
    idx_ref[...] = idx_o
    rank_ref[...] = rank_o
    gate_ref[...] = gate_o
    cnt_ref[...] = carry_ref[...]


def _route(logits, rbias, tri):
    n, ne = logits.shape
    tr = tri.shape[0]
    return pl.pallas_call(
        _route_kernel,
        out_shape=(jax.ShapeDtypeStruct((n, 128), jnp.int32), jax.ShapeDtypeStruct((n, 128), jnp.int32),
                   jax.ShapeDtypeStruct((n, 128), F32), jax.ShapeDtypeStruct((8, ne), F32)),
        grid=(n // tr,),
        in_specs=[pl.BlockSpec((tr, ne), lambda i: (i, 0)),
                  pl.BlockSpec((1, ne), lambda i: (0, 0)),
                  pl.BlockSpec((tr, tr), lambda i: (0, 0))],
        out_specs=(pl.BlockSpec((tr, 128), lambda i: (i, 0)),
                   pl.BlockSpec((tr, 128), lambda i: (i, 0)),
                   pl.BlockSpec((tr, 128), lambda i: (i, 0)),
                   pl.BlockSpec((8, ne), lambda i: (0, 0))),
        scratch_shapes=[pltpu.VMEM((8, ne), F32)],
        compiler_params=_cparams(("arbitrary",)),
        name="route",
    )(logits, rbias, tri)


def _dispatch_kernel(dest_ref, h_ref, xs_in_ref, xs_ref, sems):
    del xs_in_ref
    n_tok = h_ref.shape[0]
    n_copy = n_tok * TOP_K

    def copy(i):
        t = i // TOP_K
        return pltpu.make_async_copy(h_ref.at[pl.ds(t, 1)], xs_ref.at[pl.ds(dest_ref[i], 1)],
                                     sems.at[i % DMA_RING])

    def body(i, c):
        @pl.when(i >= DMA_RING)
        def _():
            copy(i - DMA_RING).wait()
        copy(i).start()
        return c

    lax.fori_loop(0, n_copy, body, 0)

    def drain(i, c):
        copy(i).wait()
        return c

    lax.fori_loop(n_copy - DMA_RING, n_copy, drain, 0)


def _dispatch(dest_flat, h2, xs_zero):
    return pl.pallas_call(
        _dispatch_kernel,
        out_shape=jax.ShapeDtypeStruct(xs_zero.shape, xs_zero.dtype),
        grid_spec=pltpu.PrefetchScalarGridSpec(
            num_scalar_prefetch=1,
            grid=(1,),
            in_specs=[pl.BlockSpec(memory_space=pl.ANY), pl.BlockSpec(memory_space=pl.ANY)],
            out_specs=pl.BlockSpec(memory_space=pl.ANY),
            scratch_shapes=[pltpu.SemaphoreType.DMA((DMA_RING,))]),
        input_output_aliases={2: 0},
        compiler_params=_cparams(("arbitrary",)),
        name="dispatch",
    )(dest_flat, h2, xs_zero)


def _moe_kernel(te_ref, nu_ref, x_ref, wg_ref, wu_ref, wd_ref, y_ref):
    i = pl.program_id(0)

    @pl.when(i < nu_ref[0])
    def _():
        x = x_ref[...].astype(BF16)
        hg = jnp.dot(x, wg_ref[...], preferred_element_type=F32)
        hu = jnp.dot(x, wu_ref[...], preferred_element_type=F32)
        h = (hg / (1.0 + jnp.exp(-hg))) * hu
        y_ref[...] = jnp.dot(h.astype(BF16), wd_ref[...], preferred_element_type=F32)

    @pl.when(i >= nu_ref[0])
    def _():
        y_ref[...] = jnp.zeros_like(y_ref)


def _moe(tile_expert, n_used, xs, wg, wu, wd):
    rows, d = xs.shape
    f = wg.shape[2]
    tm = MOE_TILE
    return pl.pallas_call(
        _moe_kernel,
        out_shape=jax.ShapeDtypeStruct((rows, d), F32),
        grid_spec=pltpu.PrefetchScalarGridSpec(
            num_scalar_prefetch=2,
            grid=(rows // tm,),
            in_specs=[pl.BlockSpec((tm, d), lambda i, te, nu: (i, 0)),
                      pl.BlockSpec((None, d, f), lambda i, te, nu: (te[i], 0, 0)),
                      pl.BlockSpec((None, d, f), lambda i, te, nu: (te[i], 0, 0)),
                      pl.BlockSpec((None, f, d), lambda i, te, nu: (te[i], 0, 0))],
            out_specs=pl.BlockSpec((tm, d), lambda i, te, nu: (i, 0))),
        compiler_params=_cparams(("arbitrary",), V7X_VMEM_LIMIT),
        name="moe",
    )(tile_expert, n_used, xs, wg, wu, wd)


def _shared_kernel(x_ref, wg_ref, wu_ref, wd_ref, y_ref):
    x = x_ref[...]
    hg = jnp.dot(x, wg_ref[...], preferred_element_type=F32)
    hu = jnp.dot(x, wu_ref[...], preferred_element_type=F32)
    h = (hg / (1.0 + jnp.exp(-hg))) * hu
    y_ref[...] = jnp.dot(h.astype(BF16), wd_ref[...], preferred_element_type=F32)


def _shared(h2b, wg, wu, wd):
    n, d = h2b.shape
    f = wg.shape[1]
    tm = min(512, n)
    const = lambda i: (0, 0)
    return pl.pallas_call(
        _shared_kernel,
        out_shape=jax.ShapeDtypeStruct((n, d), F32),
        grid=(n // tm,),
        in_specs=[pl.BlockSpec((tm, d), lambda i: (i, 0)),
                  pl.BlockSpec((d, f), const),
                  pl.BlockSpec((d, f), const),
                  pl.BlockSpec((f, d), const)],
        out_specs=pl.BlockSpec((tm, d), lambda i: (i, 0)),
        compiler_params=_cparams(("parallel",), V7X_VMEM_LIMIT),
        name="shared",
    )(h2b, wg, wu, wd)


def _combine_kernel(dest_ref, y_ref, gate_ref, sh_ref, x1_ref, mod_ref, g_ref, o_ref, buf_ref, sem):
    tc = x1_ref.shape[0]
    base = pl.program_id(0) * (tc * TOP_K)

    def copy(j):
        t = j // TOP_K
        k = j % TOP_K
        return pltpu.make_async_copy(y_ref.at[pl.ds(dest_ref[base + j], 1)],
                                     buf_ref.at[k, pl.ds(t, 1)], sem.at[0])

    def start(j, c):
        copy(j).start()
        return c

    def wait(j, c):
        copy(j).wait()
        return c

    lax.fori_loop(0, tc * TOP_K, start, 0)
    lax.fori_loop(0, tc * TOP_K, wait, 0)

    gate = gate_ref[...]
    y = sh_ref[...]
    for k in range(TOP_K):
        y = y + gate[:, k:k + 1] * buf_ref[k]
    o_ref[...] = x1_ref[...] + mod_ref[5:6, :] * _rms(y, g_ref[...])


def _combine(dest_flat, y, gates, shared, x1, mod, g, seq):
    n, d = x1.shape
    tc = min(COMBINE_TILE, seq)
    per_b = seq // tc
    return pl.pallas_call(
        _combine_kernel,
        out_shape=jax.ShapeDtypeStruct((n, d), F32),
        grid_spec=pltpu.PrefetchScalarGridSpec(
            num_scalar_prefetch=1,
            grid=(n // tc,),
            in_specs=[pl.BlockSpec(memory_space=pl.ANY),
                      pl.BlockSpec((tc, 128), lambda i, dest: (i, 0)),
                      pl.BlockSpec((tc, d), lambda i, dest: (i, 0)),
                      pl.BlockSpec((tc, d), lambda i, dest: (i, 0)),
                      pl.BlockSpec((None, 6, d), lambda i, dest: (i // per_b, 0, 0)),
                      pl.BlockSpec((1, d), lambda i, dest: (0, 0))],
            out_specs=pl.BlockSpec((tc, d), lambda i, dest: (i, 0)),
            scratch_shapes=[pltpu.VMEM((TOP_K, tc, d), F32),
                            pltpu.SemaphoreType.DMA((1,))]),
        compiler_params=_cparams(("arbitrary",), V7X_VMEM_LIMIT),
        name="combine",
    )(dest_flat, y, gates, shared, x1, mod, g)


def _t5_bucket(dist):
    max_exact = REL_BUCKETS // 2
    d_f = jnp.maximum(dist, max_exact).astype(F32)
    large = max_exact + (jnp.log(d_f / max_exact) / math.log(REL_MAX_DISTANCE / max_exact)
                         * (REL_BUCKETS - max_exact)).astype(jnp.int32)
    large = jnp.minimum(large, REL_BUCKETS - 1)
    return jnp.where(dist < max_exact, dist, large)


def _bucket_tables():
    i = jnp.arange(DSWA_BLK)[:, None]
    j = jnp.arange(2 * DSWA_BLK)[None, :]
    diff = jnp.maximum(i + DSWA_BLK - j, 0)
    return jnp.stack([_t5_bucket(diff * dil) for _, dil in DSWA_PATTERNS]).astype(jnp.int32)


def _layer(x, c_act_pad, rel_bias, norm_pre_attn, norm_post_attn, norm_pre_ffn, norm_post_ffn,
           w_ada, b_ada, w_in, head_gain_dswa, head_gain_sb, w_out, w_router, router_bias,
           w_gate_e, w_up_e, w_down_e, w_gate_s, w_up_s, w_down_s):
    bsz, seq, d = x.shape
    n = bsz * seq
    n_heads = d // (2 * HEAD_DIM)
    n_exp = w_router.shape[1]
    row = lambda v: v.reshape(1, -1)

    mod = _ada(c_act_pad, w_ada, row(b_ada))[:bsz].reshape(bsz, 6, d)
    x2 = x.reshape(n, d)

    qkv = _qkv(x2, mod, row(norm_pre_attn), w_in.astype(BF16), seq).reshape(bsz, seq, 3 * d)

    oa = _dswa(rel_bias.T, qkv, _bucket_tables(), row(head_gain_dswa), n_heads)
    tq = min(SB_TILE, seq)
    tri = (jnp.arange(tq)[:, None] > jnp.arange(tq)[None, :]).astype(BF16)
    ob = _sb(qkv, jnp.concatenate([tri, tri], axis=0), row(head_gain_sb), n_heads, 3 * n_heads)

    wr_hi = w_router.astype(BF16)
    wr_lo = (w_router - wr_hi.astype(F32)).astype(BF16)
    x1, h2, h2b, logits = _outproj(oa.reshape(n, -1), ob.reshape(n, -1), w_out.astype(BF16), x2, mod,
                                   row(norm_post_attn), row(norm_pre_ffn), wr_hi, wr_lo, seq)

    tr = min(ROUTE_TILE, n)
    tri_r = (jnp.arange(tr)[:, None] > jnp.arange(tr)[None, :]).astype(BF16)
    idx, rank, gates, counts = _route(logits, row(router_bias), tri_r)

    counts = counts[0].astype(jnp.int32)
    padded = (counts + MOE_TILE - 1) // MOE_TILE * MOE_TILE
    pad_end = jnp.cumsum(padded)
    pad_start = pad_end - padded
    n_rows = (n * TOP_K + n_exp * (MOE_TILE - 1) + MOE_TILE - 1) // MOE_TILE * MOE_TILE
    n_tiles = n_rows // MOE_TILE
    dest = (pad_start[idx[:, :TOP_K]] + rank[:, :TOP_K]).reshape(-1).astype(jnp.int32)
    n_used = (pad_end[-1] // MOE_TILE).astype(jnp.int32)
    tile_id = jnp.minimum(jnp.arange(n_tiles, dtype=jnp.int32), jnp.maximum(n_used - 1, 0))
    tile_expert = jnp.minimum(jnp.searchsorted(pad_end, tile_id * MOE_TILE, side='right'),
                              n_exp - 1).astype(jnp.int32)

    xs = _dispatch(dest, h2, jnp.zeros((n_rows, d), F32))
    y = _moe(tile_expert, n_used.reshape(1), xs, w_gate_e.astype(BF16), w_up_e.astype(BF16),
             w_down_e.astype(BF16))
    shared = _shared(h2b, w_gate_s.astype(BF16), w_up_s.astype(BF16), w_down_s.astype(BF16))
    out = _combine(dest, y, gates, shared, x1, mod, row(norm_post_ffn), seq)
    return out.reshape(bsz, seq, d)


def kernel(x, c, rel_bias, norm_pre_attn, norm_post_attn, norm_pre_ffn, norm_post_ffn, w_ada, b_ada, w_in, head_gain_dswa, head_gain_sb, w_out, w_router, router_bias, w_gate_e, w_up_e, w_down_e, w_gate_s, w_up_s, w_down_s):
    bsz = x.shape[0]
    depth = w_ada.shape[0]
    c_pad = jnp.zeros((8, c.shape[1]), F32).at[:bsz].set(c)
    for l in range(depth):
        x = _layer(x, c_pad, rel_bias, norm_pre_attn[l], norm_post_attn[l], norm_pre_ffn[l],
                   norm_post_ffn[l], w_ada[l], b_ada[l], w_in[l], head_gain_dswa[l], head_gain_sb[l],
                   w_out[l], w_router[l], router_bias[l], w_gate_e[l], w_up_e[l], w_down_e[l],
                   w_gate_s[l], w_up_s[l], w_down_s[l])
    return x
```

```python
import math

import jax
import jax.numpy as jnp
from jax import lax
from jax.experimental import pallas as pl
from jax.experimental.pallas import tpu as pltpu

F32 = jnp.float32
BF16 = jnp.bfloat16

HEAD_DIM = 128
DSWA_PATTERNS = ((128, 1), (512, 4), (2048, 16))
DSWA_BLK = 128
DSWA_GROUP = 4
REL_BUCKETS = 32
REL_MAX_DISTANCE = 2048
TOP_K = 6
ROUTED_SCALING = 2.446
NORM_EPS = 1e-6
NEG_BIG = -1e30
LOG2E = 1.4426950408889634

V7X_VMEM_LIMIT = 56 * 1024 * 1024
MOE_TILE = 256
SB_TILE = 256
SB_HEADS = 4
SB_ROW_SPLIT = 2
ROUTE_TILE = 512
DISPATCH_TILE = 256
COMBINE_TILE = 128
OUT_LANES = 128


def _cparams(sem, vmem=None):
    return pltpu.CompilerParams(dimension_semantics=sem, vmem_limit_bytes=vmem)


def _rms(x, g):
    return x * lax.rsqrt(jnp.mean(x * x, axis=-1, keepdims=True) + NORM_EPS) * g


def _ada_kernel(c_ref, w_ref, b_ref, o_ref):
    c = c_ref[...]
    ca = c / (1.0 + jnp.exp(-c))
    o_ref[...] = jnp.dot(ca.astype(BF16), w_ref[...].astype(BF16),
                         preferred_element_type=F32) + b_ref[...]


def _ada(c_pad, w_ada, b_ada):
    rows, d = c_pad.shape
    n = w_ada.shape[1]
    tn = min(n, 1536)
    return pl.pallas_call(
        _ada_kernel,
        out_shape=jax.ShapeDtypeStruct((rows, n), F32),
        grid=(n // tn,),
        in_specs=[pl.BlockSpec((rows, d), lambda j: (0, 0)),
                  pl.BlockSpec((d, tn), lambda j: (0, j)),
                  pl.BlockSpec((1, tn), lambda j: (0, j))],
        out_specs=pl.BlockSpec((rows, tn), lambda j: (0, j)),
        compiler_params=_cparams(("arbitrary",), V7X_VMEM_LIMIT),
        name="ada",
    )(c_pad, w_ada, b_ada)


def _qkv_kernel(x_ref, mod_ref, g_ref, w_ref, o_ref, h_ref):
    @pl.when(pl.program_id(1) == 0)
    def _():
        h = _rms(x_ref[...], g_ref[...]) * (1.0 + mod_ref[1:2, :]) + mod_ref[0:1, :]
        h_ref[...] = h.astype(BF16)

    o_ref[...] = jnp.dot(h_ref[...], w_ref[...], preferred_element_type=F32).astype(o_ref.dtype)


def _qkv(x2, mod, g, w_bf, seq, col_lo, col_hi, out_dtype):
    n, d = x2.shape
    nout = col_hi - col_lo
    tm = min(512, seq)
    tn = min(1024, nout)
    j0 = col_lo // tn
    per_b = seq // tm
    return pl.pallas_call(
        _qkv_kernel,
        out_shape=jax.ShapeDtypeStruct((n, nout), out_dtype),
        grid=(n // tm, nout // tn),
        in_specs=[pl.BlockSpec((tm, d), lambda i, j: (i, 0)),
                  pl.BlockSpec((None, 6, d), lambda i, j: (i // per_b, 0, 0)),
                  pl.BlockSpec((1, d), lambda i, j: (0, 0)),
                  pl.BlockSpec((d, tn), lambda i, j: (0, j0 + j))],
        out_specs=pl.BlockSpec((tm, tn), lambda i, j: (i, j)),
        scratch_shapes=[pltpu.VMEM((tm, d), BF16)],
        compiler_params=_cparams(("parallel", "arbitrary"), V7X_VMEM_LIMIT),
        name="qkv",
    )(x2, mod, g, w_bf)


def _dswa_kernel(rb_ref, q_ref, k_ref, v_ref, bucket_ref, g_ref, o_ref, *stats):
    n_pat = len(DSWA_PATTERNS)
    acc_refs, m_refs, l_refs = stats[:n_pat], stats[n_pat:2 * n_pat], stats[2 * n_pat:]
    seq = q_ref.shape[0]
    head = pl.program_id(1)
    scale = 1.0 / math.sqrt(HEAD_DIM)
    blk = DSWA_BLK
    ii = lax.broadcasted_iota(jnp.int32, (blk, 2 * blk), 0)
    jj = lax.broadcasted_iota(jnp.int32, (blk, 2 * blk), 1)
    diff = ii + blk - jj
    band = (diff >= 0) & (diff <= blk)

    for p, (_, dil) in enumerate(DSWA_PATTERNS):
        nb = seq // dil // blk
        bucket = bucket_ref[p]
        bias = jnp.zeros((blk, 2 * blk), F32)
        for b in range(REL_BUCKETS):
            bias = jnp.where(bucket == b, rb_ref[head, b], bias)
        bias = jnp.where(band, bias, NEG_BIG)
        acc_ref, m_ref, l_ref = acc_refs[p], m_refs[p], l_refs[p]

        def rows(start, size, dil=dil):
            if dil == 1:
                return pl.ds(start, size)
            return pl.ds(start, size, stride=dil)

        def group(blocks, bias=bias, rows=rows, dil=dil,
                  acc_ref=acc_ref, m_ref=m_ref, l_ref=l_ref):
            rqs, rks, bss = [], [], []
            for r, n in blocks:
                rq = rows(n * blk * dil + r, blk)
                rqs.append(rq)
                rks.append(rq if n == 0 else rows((n - 1) * blk * dil + r, 2 * blk))
                bss.append(bias[:, blk:] if n == 0 else bias)
            qs = [(q_ref[rq, :] * scale).astype(BF16) for rq in rqs]
            ks = [k_ref[rk, :].astype(BF16) for rk in rks]
            ss = [lax.dot_general(q, k, (((1,), (1,)), ((), ())), preferred_element_type=F32) + bs
                  for q, k, bs in zip(qs, ks, bss)]
            ms = [jnp.max(s, axis=-1, keepdims=True) for s in ss]
            ps = [jnp.exp(s - m) for s, m in zip(ss, ms)]
            vs = [v_ref[rk, :].astype(BF16) for rk in rks]
            for rq, m, pexp, v in zip(rqs, ms, ps, vs):
                acc_ref[rq, :] = jnp.dot(pexp.astype(BF16), v, preferred_element_type=F32)
                m_ref[rq, :] = jnp.broadcast_to(m, (blk, HEAD_DIM))
                l_ref[rq, :] = jnp.broadcast_to(jnp.sum(pexp, axis=-1, keepdims=True),
                                                (blk, HEAD_DIM))

        per_phase = min(nb, DSWA_GROUP)
        phases = max(1, min(dil, DSWA_GROUP // per_phase))

        def trip(t, carry, nb=nb, per_phase=per_phase, phases=phases, group=group):
            for n0 in range(0, nb, per_phase):
                group([(t * phases + dr, n0 + dn) for dr in range(phases) for dn in range(per_phase)])
            return carry

        lax.fori_loop(0, dil // phases, trip, 0)

    chunk = min(256, seq)

    def finish(c, carry):
        rs = pl.ds(pl.multiple_of(c * chunk, chunk), chunk)
        ms = [m_ref[rs, :] for m_ref in m_refs]
        m_all = ms[0]
        for m in ms[1:]:
            m_all = jnp.maximum(m_all, m)
        num = jnp.zeros((chunk, HEAD_DIM), F32)
        den = jnp.zeros((chunk, HEAD_DIM), F32)
        for p in range(n_pat):
            w = jnp.exp(ms[p] - m_all)
            num = num + w * acc_refs[p][rs, :]
            den = den + w * l_refs[p][rs, :]
        o_ref[rs, :] = _rms(num / den, g_ref[...]).astype(o_ref.dtype)
        return carry

    lax.fori_loop(0, seq // chunk, finish, 0)


def _dswa(rel_bias_t, qkv3, buckets, gain, n_heads):
    bsz, seq, _ = qkv3.shape
    w = n_heads * HEAD_DIM
    return pl.pallas_call(
        _dswa_kernel,
        out_shape=jax.ShapeDtypeStruct((bsz, seq, w), BF16),
        grid=(bsz, n_heads),
        in_specs=[pl.BlockSpec(memory_space=pltpu.SMEM),
                  pl.BlockSpec((None, seq, HEAD_DIM), lambda b, h: (b, 0, h)),
                  pl.BlockSpec((None, seq, HEAD_DIM), lambda b, h: (b, 0, n_heads + h)),
                  pl.BlockSpec((None, seq, HEAD_DIM), lambda b, h: (b, 0, 2 * n_heads + h)),
                  pl.BlockSpec(buckets.shape, lambda b, h: (0, 0, 0)),
                  pl.BlockSpec((1, HEAD_DIM), lambda b, h: (0, h))],
        out_specs=pl.BlockSpec((None, seq, HEAD_DIM), lambda b, h: (b, 0, h)),
        scratch_shapes=[pltpu.VMEM((seq, HEAD_DIM), F32)] * (3 * len(DSWA_PATTERNS)),
        compiler_params=_cparams(("parallel", "parallel"), V7X_VMEM_LIMIT),
        name="dswa",
    )(rel_bias_t, qkv3, qkv3, qkv3, buckets, gain)


def _neg_abs(z):
    bits = lax.bitcast_convert_type(z, jnp.uint32) | jnp.uint32(0x80000000)
    return lax.bitcast_convert_type(bits, F32)


def _sb_kernel(q_ref, kb_ref, vb_ref, tri_ref, g_ref, o_ref):
    tq = q_ref.shape[0]
    nh = q_ref.shape[1] // HEAD_DIM
    iq = pl.program_id(2)

    q_all = (q_ref[...] * (LOG2E / math.sqrt(HEAD_DIM))).astype(BF16)
    rs = tq // SB_ROW_SPLIT
    ti = lax.broadcasted_iota(jnp.int32, (tq, tq), 0)
    si = lax.broadcasted_iota(jnp.int32, (tq, tq), 1)
    causal = si < ti
    chains = [(h, s) for h in range(nh) for s in range(SB_ROW_SPLIT)]
    cols = [slice(h * HEAD_DIM, (h + 1) * HEAD_DIM) for h, _ in chains]
    rws = [slice(s * rs, (s + 1) * rs) for _, s in chains]
    qs = [q_all[r, c] for r, c in zip(rws, cols)]
    masks = [causal[r, :] for r in rws]

    def block(jk, state, masked):
        r0 = pl.multiple_of(jk * tq, tq)
        carries, accs = state
        ks = [kb_ref[pl.ds(r0, tq), c] for c in cols]
        vs = [vb_ref[pl.ds(r0, tq), c] for c in cols]
        zs = [lax.dot_general(q, k, (((1,), (1,)), ((), ())), preferred_element_type=F32)
              for q, k in zip(qs, ks)]
        sps = [jnp.maximum(z, 0.0) + jnp.log(1.0 + jnp.exp2(_neg_abs(z))) * LOG2E for z in zs]
        if masked:
            sps = [jnp.where(m, sp, 0.0) for m, sp in zip(masks, sps)]
        his = [sp.astype(BF16) for sp in sps]
        los = [(sp - hi.astype(F32)).astype(BF16) for sp, hi in zip(sps, his)]
        afters = [jnp.dot(jnp.concatenate([hi, lo], axis=1), tri_ref[...], preferred_element_type=F32)
                  for hi, lo in zip(his, los)]
        probs = [jnp.exp2(z - sp + after + carry)
                 for z, sp, after, carry in zip(zs, sps, afters, carries)]
        if masked:
            probs = [jnp.where(m, a, 0.0) for m, a in zip(masks, probs)]
        accs = tuple(acc + jnp.dot(a.astype(BF16), v, preferred_element_type=F32)
                     for acc, a, v in zip(accs, probs, vs))
        carries = tuple(carry - jnp.sum(sp, axis=-1, keepdims=True)
                        for carry, sp in zip(carries, sps))
        return carries, accs

    state = (tuple(jnp.zeros((rs, 1), F32) for _ in chains),
             tuple(jnp.zeros((rs, HEAD_DIM), F32) for _ in chains))
    state = block(iq, state, True)
    state = lax.fori_loop(0, iq, lambda t, st: block(iq - 1 - t, st, False), state)
    for r, c, acc in zip(rws, cols, state[1]):
        o_ref[r, c] = _rms(acc, g_ref[:, c]).astype(o_ref.dtype)


def _sb(q_src, kv, tri2, gain, n_heads, q_col0):
    bsz, seq, _ = kv.shape
    tq = min(SB_TILE, seq)
    nh = min(SB_HEADS, n_heads)
    wblk = nh * HEAD_DIM
    groups = n_heads // nh
    c0 = q_col0 // nh
    return pl.pallas_call(
        _sb_kernel,
        out_shape=jax.ShapeDtypeStruct((bsz, seq, n_heads * HEAD_DIM), BF16),
        grid=(bsz, groups, seq // tq),
        in_specs=[pl.BlockSpec((None, tq, wblk), lambda b, h, i: (b, i, c0 + h)),
                  pl.BlockSpec((None, seq, wblk), lambda b, h, i: (b, 0, h)),
                  pl.BlockSpec((None, seq, wblk), lambda b, h, i: (b, 0, groups + h)),
                  pl.BlockSpec(tri2.shape, lambda b, h, i: (0, 0)),
                  pl.BlockSpec((1, wblk), lambda b, h, i: (0, h))],
        out_specs=pl.BlockSpec((None, tq, wblk), lambda b, h, i: (b, i, h)),
        compiler_params=_cparams(("parallel", "parallel", "arbitrary"), V7X_VMEM_LIMIT),
        name="sb",
    )(q_src, kv, kv, tri2, gain)


def _outproj_kernel(oa_ref, ob_ref, w_ref, x_ref, mod_ref, gpa_ref, gpf_ref, wr_hi_ref, wr_lo_ref,
                    x1_ref, h2_ref, h2b_ref, lg_ref):
    wa = oa_ref.shape[1]
    o = jnp.dot(oa_ref[...], w_ref[0:wa, :], preferred_element_type=F32)
    o = o + jnp.dot(ob_ref[...], w_ref[wa:, :], preferred_element_type=F32)
    x1 = x_ref[...] + mod_ref[2:3, :] * _rms(o, gpa_ref[...])
    x1_ref[...] = x1
    h2 = _rms(x1, gpf_ref[...]) * (1.0 + mod_ref[4:5, :]) + mod_ref[3:4, :]
    h2_ref[...] = h2
    hi = h2.astype(BF16)
    h2b_ref[...] = hi
    lo = (h2 - hi.astype(F32)).astype(BF16)
    lg = jnp.dot(hi, wr_hi_ref[...], preferred_element_type=F32)
    lg = lg + jnp.dot(lo, wr_hi_ref[...], preferred_element_type=F32)
    lg = lg + jnp.dot(hi, wr_lo_ref[...], preferred_element_type=F32)
    lg_ref[...] = lg


def _outproj(oa, ob, w_bf, x2, mod, gpa, gpf, wr_hi, wr_lo, seq):
    n, d = x2.shape
    wa = oa.shape[1]
    ne = wr_hi.shape[1]
    tm = min(256, seq)
    per_b = seq // tm
    const = lambda i: (0, 0)
    return pl.pallas_call(
        _outproj_kernel,
        out_shape=(jax.ShapeDtypeStruct((n, d), F32), jax.ShapeDtypeStruct((n, d), F32),
                   jax.ShapeDtypeStruct((n, d), BF16), jax.ShapeDtypeStruct((n, ne), F32)),
        grid=(n // tm,),
        in_specs=[pl.BlockSpec((tm, wa), lambda i: (i, 0)),
                  pl.BlockSpec((tm, wa), lambda i: (i, 0)),
                  pl.BlockSpec(w_bf.shape, const),
                  pl.BlockSpec((tm, d), lambda i: (i, 0)),
                  pl.BlockSpec((None, 6, d), lambda i: (i // per_b, 0, 0)),
                  pl.BlockSpec((1, d), const),
                  pl.BlockSpec((1, d), const),
                  pl.BlockSpec(wr_hi.shape, const),
                  pl.BlockSpec(wr_lo.shape, const)],
        out_specs=(pl.BlockSpec((tm, d), lambda i: (i, 0)),
                   pl.BlockSpec((tm, d), lambda i: (i, 0)),
                   pl.BlockSpec((tm, d), lambda i: (i, 0)),
                   pl.BlockSpec((tm, ne), lambda i: (i, 0))),
        compiler_params=_cparams(("parallel",), V7X_VMEM_LIMIT),
        name="outproj",
    )(oa, ob, w_bf, x2, mod, gpa, gpf, wr_hi, wr_lo)


def _topk(lg, rbias):
    tr, ne = lg.shape
    scores = 1.0 / (1.0 + jnp.exp(-lg))
    sel = scores + rbias
    lane = lax.broadcasted_iota(jnp.int32, (tr, ne), 1).astype(F32)
    onehots, gates = [], []
    chosen = jnp.zeros((tr, ne), F32)
    for _ in range(TOP_K):
        m = jnp.max(sel, axis=-1, keepdims=True)
        idx = jnp.min(jnp.where(sel == m, lane, float(ne)), axis=-1, keepdims=True)
        oh = lane == idx
        gates.append(jnp.sum(jnp.where(oh, scores, 0.0), axis=-1, keepdims=True))
        sel = jnp.where(oh, -jnp.inf, sel)
        chosen = jnp.where(oh, 1.0, chosen)
        onehots.append(oh)
    return onehots, gates, chosen


def _route_count_kernel(lg_ref, rbias_ref, cnt_ref):
    @pl.when(pl.program_id(0) == 0)
    def _():
        cnt_ref[...] = jnp.zeros_like(cnt_ref)

    _, _, chosen = _topk(lg_ref[...], rbias_ref[...])
    cnt_ref[0:1, :] = cnt_ref[0:1, :] + jnp.sum(chosen, axis=0, keepdims=True)


def _route_count(logits, rbias, tr):
    n, ne = logits.shape
    return pl.pallas_call(
        _route_count_kernel,
        out_shape=jax.ShapeDtypeStruct((8, ne), F32),
        grid=(n // tr,),
        in_specs=[pl.BlockSpec((tr, ne), lambda i: (i, 0)),
                  pl.BlockSpec((1, ne), lambda i: (0, 0))],
        out_specs=pl.BlockSpec((8, ne), lambda i: (0, 0)),
        compiler_params=_cparams(("arbitrary",)),
        name="route_count",
    )(logits, rbias)


def _route_assign_kernel(lg_ref, rbias_ref, tri_ref, start_ref, dest_ref, gate_ref, seen_ref):
    tr = lg_ref.shape[0]

    @pl.when(pl.program_id(0) == 0)
    def _():
        seen_ref[...] = jnp.zeros_like(seen_ref)

    onehots, gates, chosen = _topk(lg_ref[...], rbias_ref[...])
    gsum = gates[0]
    for g in gates[1:]:
        gsum = gsum + g
    before = jnp.dot(tri_ref[...], chosen.astype(BF16), preferred_element_type=F32)
    slot = before + seen_ref[0:1, :] + start_ref[...]
    seen_ref[0:1, :] = seen_ref[0:1, :] + jnp.sum(chosen, axis=0, keepdims=True)

    out_lane = lax.broadcasted_iota(jnp.int32, (tr, OUT_LANES), 1)
    dest_o = jnp.zeros((tr, OUT_LANES), jnp.int32)
    gate_o = jnp.zeros((tr, OUT_LANES), F32)
    for k in range(TOP_K):
        dest = jnp.sum(jnp.where(onehots[k], slot, 0.0), axis=-1, keepdims=True)
        dest_o = jnp.where(out_lane == k, dest.astype(jnp.int32), dest_o)
        gate_o = jnp.where(out_lane == k, gates[k] / gsum * ROUTED_SCALING, gate_o)
    dest_ref[...] = dest_o
    gate_ref[...] = gate_o


def _route_assign(logits, rbias, tri, start):
    n, ne = logits.shape
    tr = tri.shape[0]
    return pl.pallas_call(
        _route_assign_kernel,
        out_shape=(jax.ShapeDtypeStruct((n, OUT_LANES), jnp.int32),
                   jax.ShapeDtypeStruct((n, OUT_LANES), F32)),
        grid=(n // tr,),
        in_specs=[pl.BlockSpec((tr, ne), lambda i: (i, 0)),
                  pl.BlockSpec((1, ne), lambda i: (0, 0)),
                  pl.BlockSpec((tr, tr), lambda i: (0, 0)),
                  pl.BlockSpec((1, ne), lambda i: (0, 0))],
        out_specs=(pl.BlockSpec((tr, OUT_LANES), lambda i: (i, 0)),
                   pl.BlockSpec((tr, OUT_LANES), lambda i: (i, 0))),
        scratch_shapes=[pltpu.VMEM((8, ne), F32)],
        compiler_params=_cparams(("arbitrary",)),
        name="route_assign",
    )(logits, rbias, tri, start)


def _dispatch_kernel(dest_ref, lo_ref, hi_ref, h_ref, xs_ref, zero_ref, sem, zsem):
    td = h_ref.shape[0]
    step = pl.program_id(0)
    n_exp = lo_ref.shape[0]
    tile = zero_ref.shape[0]

    @pl.when(step == 0)
    def _():
        zero_ref[...] = jnp.zeros_like(zero_ref)

        def zcopy(p):
            return pltpu.make_async_copy(zero_ref.at[pl.ds(0, 1)], xs_ref.at[pl.ds(p, 1)], zsem.at[0])

        def fill(e, c):
            lax.fori_loop(lo_ref[e], hi_ref[e], lambda p, c2: (zcopy(p).start(), c2)[1], 0)
            lax.fori_loop(lo_ref[e], hi_ref[e], lambda p, c2: (zcopy(p).wait(), c2)[1], 0)
            return c

        lax.fori_loop(0, n_exp, fill, 0)

        def tcopy(t):
            rows = pl.ds(pl.multiple_of(t * tile, tile), tile)
            return pltpu.make_async_copy(zero_ref, xs_ref.at[rows], zsem.at[1])

        first, last = hi_ref[n_exp - 1] // tile, xs_ref.shape[0] // tile
        lax.fori_loop(first, last, lambda t, c2: (tcopy(t).start(), c2)[1], 0)
        lax.fori_loop(first, last, lambda t, c2: (tcopy(t).wait(), c2)[1], 0)

    base = step * (td * TOP_K)

    def copy(t, k):
        return pltpu.make_async_copy(h_ref.at[pl.ds(t, 1)],
                                     xs_ref.at[pl.ds(dest_ref[base + t * TOP_K + k], 1)], sem.at[0])

    def start(t, c):
        for k in range(TOP_K):
            copy(t, k).start()
        return c

    def wait(t, c):
        for k in range(TOP_K):
            copy(t, k).wait()
        return c

    lax.fori_loop(0, td, start, 0)
    lax.fori_loop(0, td, wait, 0)


def _dispatch(dest_flat, fill_lo, fill_hi, h2, n_rows):
    n, d = h2.shape
    td = min(DISPATCH_TILE, n)
    return pl.pallas_call(
        _dispatch_kernel,
        out_shape=jax.ShapeDtypeStruct((n_rows, d), h2.dtype),
        grid_spec=pltpu.PrefetchScalarGridSpec(
            num_scalar_prefetch=3,
            grid=(n // td,),
            in_specs=[pl.BlockSpec((td, d), lambda i, *_: (i, 0))],
            out_specs=pl.BlockSpec(memory_space=pl.ANY),
            scratch_shapes=[pltpu.VMEM((MOE_TILE, d), h2.dtype),
                            pltpu.SemaphoreType.DMA((1,)),
                            pltpu.SemaphoreType.DMA((2,))]),
        compiler_params=_cparams(("arbitrary",)),
        name="dispatch",
    )(dest_flat, fill_lo, fill_hi, h2)


def _moe_kernel(te_ref, nu_ref, x_ref, wg_ref, wu_ref, wd_ref, y_ref):
    i = pl.program_id(0)

    @pl.when(i < nu_ref[0])
    def _():
        x = x_ref[...].astype(BF16)
        hg = jnp.dot(x, wg_ref[...], preferred_element_type=F32)
        hu = jnp.dot(x, wu_ref[...], preferred_element_type=F32)
        h = (hg / (1.0 + jnp.exp(-hg))) * hu
        y_ref[...] = jnp.dot(h.astype(BF16), wd_ref[...], preferred_element_type=F32)

    @pl.when(i >= nu_ref[0])
    def _():
        y_ref[...] = jnp.zeros_like(y_ref)


def _moe(tile_expert, n_used, xs, wg, wu, wd):
    rows, d = xs.shape
    f = wg.shape[2]
    tm = MOE_TILE
    return pl.pallas_call(
        _moe_kernel,
        out_shape=jax.ShapeDtypeStruct((rows, d), F32),
        grid_spec=pltpu.PrefetchScalarGridSpec(
            num_scalar_prefetch=2,
            grid=(rows // tm,),
            in_specs=[pl.BlockSpec((tm, d), lambda i, te, nu: (jnp.minimum(i, nu[0] - 1), 0)),
                      pl.BlockSpec((None, d, f), lambda i, te, nu: (te[i], 0, 0)),
                      pl.BlockSpec((None, d, f), lambda i, te, nu: (te[i], 0, 0)),
                      pl.BlockSpec((None, f, d), lambda i, te, nu: (te[i], 0, 0))],
            out_specs=pl.BlockSpec((tm, d), lambda i, te, nu: (i, 0))),
        compiler_params=_cparams(("arbitrary",), V7X_VMEM_LIMIT),
        name="moe",
    )(tile_expert, n_used, xs, wg, wu, wd)


def _shared_kernel(x_ref, wg_ref, wu_ref, wd_ref, y_ref):
    x = x_ref[...]
    hg = jnp.dot(x, wg_ref[...], preferred_element_type=F32)
    hu = jnp.dot(x, wu_ref[...], preferred_element_type=F32)
    h = (hg / (1.0 + jnp.exp(-hg))) * hu
    y_ref[...] = jnp.dot(h.astype(BF16), wd_ref[...], preferred_element_type=F32)


def _shared(h2b, wg, wu, wd):
    n, d = h2b.shape
    f = wg.shape[1]
    tm = min(512, n)
    const = lambda i: (0, 0)
    return pl.pallas_call(
        _shared_kernel,
        out_shape=jax.ShapeDtypeStruct((n, d), F32),
        grid=(n // tm,),
        in_specs=[pl.BlockSpec((tm, d), lambda i: (i, 0)),
                  pl.BlockSpec((d, f), const),
                  pl.BlockSpec((d, f), const),
                  pl.BlockSpec((f, d), const)],
        out_specs=pl.BlockSpec((tm, d), lambda i: (i, 0)),
        compiler_params=_cparams(("parallel",), V7X_VMEM_LIMIT),
        name="shared",
    )(h2b, wg, wu, wd)


def _combine_kernel(dest_ref, y_ref, gate_ref, sh_ref, x1_ref, mod_ref, g_ref, o_ref, buf_ref, sem):
    tc = x1_ref.shape[0]
    step = pl.program_id(0)
    n_steps = pl.num_programs(0)

    def copy(tile, slot, t, k):
        src = dest_ref[(tile * tc + t) * TOP_K + k]
        return pltpu.make_async_copy(y_ref.at[pl.ds(src, 1)], buf_ref.at[slot, k, pl.ds(t, 1)],
                                     sem.at[slot])

    def request(tile, slot):
        def body(t, c):
            for k in range(TOP_K):
                copy(tile, slot, t, k).start()
            return c
        lax.fori_loop(0, tc, body, 0)

    slot = step % 2

    @pl.when(step == 0)
    def _():
        request(step, slot)

    @pl.when(step + 1 < n_steps)
    def _():
        request(step + 1, 1 - slot)

    def await_(t, c):
        for k in range(TOP_K):
            copy(step, slot, t, k).wait()
        return c

    lax.fori_loop(0, tc, await_, 0)

    gate = gate_ref[...]
    y = sh_ref[...]
    for k in range(TOP_K):
        y = y + gate[:, k:k + 1] * buf_ref[slot, k]
    o_ref[...] = x1_ref[...] + mod_ref[5:6, :] * _rms(y, g_ref[...])


def _combine(dest_flat, y, gates, shared, x1, mod, g, seq):
    n, d = x1.shape
    tc = min(COMBINE_TILE, seq)
    per_b = seq // tc
    return pl.pallas_call(
        _combine_kernel,
        out_shape=jax.ShapeDtypeStruct((n, d), F32),
        grid_spec=pltpu.PrefetchScalarGridSpec(
            num_scalar_prefetch=1,
            grid=(n // tc,),
            in_specs=[pl.BlockSpec(memory_space=pl.ANY),
                      pl.BlockSpec((tc, OUT_LANES), lambda i, dest: (i, 0)),
                      pl.BlockSpec((tc, d), lambda i, dest: (i, 0)),
                      pl.BlockSpec((tc, d), lambda i, dest: (i, 0)),
                      pl.BlockSpec((None, 6, d), lambda i, dest: (i // per_b, 0, 0)),
                      pl.BlockSpec((1, d), lambda i, dest: (0, 0))],
            out_specs=pl.BlockSpec((tc, d), lambda i, dest: (i, 0)),
            scratch_shapes=[pltpu.VMEM((2, TOP_K, tc, d), F32),
                            pltpu.SemaphoreType.DMA((2,))]),
        compiler_params=_cparams(("arbitrary",), V7X_VMEM_LIMIT),
        name="combine",
    )(dest_flat, y, gates, shared, x1, mod, g)


def _t5_bucket(dist):
    max_exact = REL_BUCKETS // 2
    d_f = jnp.maximum(dist, max_exact).astype(F32)
    large = max_exact + (jnp.log(d_f / max_exact) / math.log(REL_MAX_DISTANCE / max_exact)
                         * (REL_BUCKETS - max_exact)).astype(jnp.int32)
    large = jnp.minimum(large, REL_BUCKETS - 1)
    return jnp.where(dist < max_exact, dist, large)


def _bucket_tables():
    i = jnp.arange(DSWA_BLK)[:, None]
    j = jnp.arange(2 * DSWA_BLK)[None, :]
    diff = jnp.maximum(i + DSWA_BLK - j, 0)
    return jnp.stack([_t5_bucket(diff * dil) for _, dil in DSWA_PATTERNS]).astype(jnp.int32)


def _strict_lower(n):
    return jnp.arange(n)[:, None] > jnp.arange(n)[None, :]


def _layer(x, c_pad, rel_bias, norm_pre_attn, norm_post_attn, norm_pre_ffn, norm_post_ffn,
           w_ada, b_ada, w_in, head_gain_dswa, head_gain_sb, w_out, w_router, router_bias,
           w_gate_e, w_up_e, w_down_e, w_gate_s, w_up_s, w_down_s):
    bsz, seq, d = x.shape
    n = bsz * seq
    n_heads = d // (2 * HEAD_DIM)
    n_exp = w_router.shape[1]
    row = lambda v: v.reshape(1, -1)

    mod = _ada(c_pad, w_ada, row(b_ada))[:bsz].reshape(bsz, 6, d)
    x2 = x.reshape(n, d)

    w_in_bf = w_in.astype(BF16)
    qkvq = _qkv(x2, mod, row(norm_pre_attn), w_in_bf, seq, 0, 2 * d, F32).reshape(bsz, seq, 2 * d)
    kv_sb = _qkv(x2, mod, row(norm_pre_attn), w_in_bf, seq, 2 * d, 3 * d, BF16).reshape(bsz, seq, d)

    oa = _dswa(rel_bias.T, qkvq, _bucket_tables(), row(head_gain_dswa), n_heads)
    tq = min(SB_TILE, seq)
    neg_tri = jnp.where(_strict_lower(tq), -1.0, 0.0).astype(BF16)
    ob = _sb(qkvq, kv_sb, jnp.concatenate([neg_tri, neg_tri], axis=0), row(head_gain_sb), n_heads,
             3 * n_heads)

    wr_hi = w_router.astype(BF16)
    wr_lo = (w_router - wr_hi.astype(F32)).astype(BF16)
    x1, h2, h2b, logits = _outproj(oa.reshape(n, -1), ob.reshape(n, -1), w_out.astype(BF16), x2, mod,
                                   row(norm_post_attn), row(norm_pre_ffn), wr_hi, wr_lo, seq)

    tr = min(ROUTE_TILE, n)
    counts = _route_count(logits, row(router_bias), tr)[0].astype(jnp.int32)
    padded = (counts + MOE_TILE - 1) // MOE_TILE * MOE_TILE
    pad_end = jnp.cumsum(padded)
    pad_start = pad_end - padded
    n_rows = (n * TOP_K + n_exp * (MOE_TILE - 1) + MOE_TILE - 1) // MOE_TILE * MOE_TILE
    n_tiles = n_rows // MOE_TILE
    n_used = jnp.maximum(pad_end[-1] // MOE_TILE, 1).astype(jnp.int32)
    tile_id = jnp.minimum(jnp.arange(n_tiles, dtype=jnp.int32), n_used - 1)
    tile_expert = jnp.minimum(
        jnp.sum(pad_end[None, :] <= (tile_id * MOE_TILE)[:, None], axis=1), n_exp - 1).astype(jnp.int32)

    dest, gates = _route_assign(logits, row(router_bias), _strict_lower(tr).astype(BF16),
                                row(pad_start.astype(F32)))
    dest_flat = dest[:, :TOP_K].reshape(-1)

    xs = _dispatch(dest_flat, (pad_start + counts).astype(jnp.int32), pad_end.astype(jnp.int32), h2, n_rows)
    y = _moe(tile_expert, n_used.reshape(1), xs, w_gate_e.astype(BF16), w_up_e.astype(BF16),
             w_down_e.astype(BF16))
    shared = _shared(h2b, w_gate_s.astype(BF16), w_up_s.astype(BF16), w_down_s.astype(BF16))
    out = _combine(dest_flat, y, gates, shared, x1, mod, row(norm_post_ffn), seq)
    return out.reshape(bsz, seq, d)


def kernel(x, c, rel_bias, norm_pre_attn, norm_post_attn, norm_pre_ffn, norm_post_ffn, w_ada, b_ada, w_in, head_gain_dswa, head_gain_sb, w_out, w_router, router_bias, w_gate_e, w_up_e, w_down_e, w_gate_s, w_up_s, w_down_s):
    bsz = x.shape[0]
    depth = w_ada.shape[0]
    c_pad = jnp.zeros((8, c.shape[1]), F32).at[:bsz].set(c)
    for l in range(depth):
        x = _layer(x, c_pad, rel_bias, norm_pre_attn[l], norm_post_attn[l], norm_pre_ffn[l],
                   norm_post_ffn[l], w_ada[l], b_ada[l], w_in[l], head_gain_dswa[l], head_gain_sb[l],
                   w_out[l], w_router[l], router_bias[l], w_gate_e[l], w_up_e[l], w_down_e[l],
                   w_gate_s[l], w_up_s[l], w_down_s[l])
    return x
```

```python
import math

import jax
import jax.numpy as jnp
from jax import lax
from jax.experimental import pallas as pl
from jax.experimental.pallas import tpu as pltpu

F32 = jnp.float32
BF16 = jnp.bfloat16

HEAD_DIM = 128
DSWA_PATTERNS = ((128, 1), (512, 4), (2048, 16))
DSWA_BLK = 128
DSWA_GROUP = 4
REL_BUCKETS = 32
REL_MAX_DISTANCE = 2048
TOP_K = 6
ROUTED_SCALING = 2.446
NORM_EPS = 1e-6
NEG_BIG = -1e30
LOG2E = 1.4426950408889634

V7X_VMEM_LIMIT = 56 * 1024 * 1024
V7X_VMEM_LIMIT_MOE = 60 * 1024 * 1024
MOE_TILE = 256
MOE_GU_CHUNKS = 8
MOE_DOWN_CHUNKS = 8
SB_TILE = 256
SB_HEADS = 4
SB_ROW_SPLIT = 2
SB_UNDERFLOW_LOG2 = -160.0
ROUTE_TILE = 512
DISPATCH_TILE = 256
COMBINE_TILE = 128
OUT_LANES = 128


def _cparams(sem, vmem=None):
    return pltpu.CompilerParams(dimension_semantics=sem, vmem_limit_bytes=vmem)


def _rms(x, g):
    return x * lax.rsqrt(jnp.mean(x * x, axis=-1, keepdims=True) + NORM_EPS) * g


def _ada_kernel(c_ref, w_ref, b_ref, o_ref):
    c = c_ref[...]
    ca = c / (1.0 + jnp.exp(-c))
    o_ref[...] = jnp.dot(ca.astype(BF16), w_ref[...].astype(BF16),
                         preferred_element_type=F32) + b_ref[...]


def _ada(c_pad, w_ada, b_ada):
    rows, d = c_pad.shape
    n = w_ada.shape[1]
    tn = min(n, 1536)
    return pl.pallas_call(
        _ada_kernel,
        out_shape=jax.ShapeDtypeStruct((rows, n), F32),
        grid=(n // tn,),
        in_specs=[pl.BlockSpec((rows, d), lambda j: (0, 0)),
                  pl.BlockSpec((d, tn), lambda j: (0, j)),
                  pl.BlockSpec((1, tn), lambda j: (0, j))],
        out_specs=pl.BlockSpec((rows, tn), lambda j: (0, j)),
        compiler_params=_cparams(("arbitrary",), V7X_VMEM_LIMIT),
        name="ada",
    )(c_pad, w_ada, b_ada)


def _qkv_kernel(x_ref, mod_ref, g_ref, w_ref, o_ref, h_ref):
    @pl.when(pl.program_id(1) == 0)
    def _():
        h = _rms(x_ref[...], g_ref[...]) * (1.0 + mod_ref[1:2, :]) + mod_ref[0:1, :]
        h_ref[...] = h.astype(BF16)

    o_ref[...] = jnp.dot(h_ref[...], w_ref[...], preferred_element_type=F32).astype(o_ref.dtype)


def _qkv(x2, mod, g, w_bf, seq, col_lo, col_hi, out_dtype):
    n, d = x2.shape
    nout = col_hi - col_lo
    tm = min(1024, seq)
    tn = min(1024, nout)
    j0 = col_lo // tn
    per_b = seq // tm
    return pl.pallas_call(
        _qkv_kernel,
        out_shape=jax.ShapeDtypeStruct((n, nout), out_dtype),
        grid=(n // tm, nout // tn),
        in_specs=[pl.BlockSpec((tm, d), lambda i, j: (i, 0)),
                  pl.BlockSpec((None, 6, d), lambda i, j: (i // per_b, 0, 0)),
                  pl.BlockSpec((1, d), lambda i, j: (0, 0)),
                  pl.BlockSpec((d, tn), lambda i, j: (0, j0 + j))],
        out_specs=pl.BlockSpec((tm, tn), lambda i, j: (i, j)),
        scratch_shapes=[pltpu.VMEM((tm, d), BF16)],
        compiler_params=_cparams(("parallel", "arbitrary"), V7X_VMEM_LIMIT),
        name="qkv",
    )(x2, mod, g, w_bf)


def _dswa_kernel(rb_ref, q_ref, k_ref, v_ref, bucket_ref, g_ref, o_ref, *stats):
    n_pat = len(DSWA_PATTERNS)
    acc_refs, m_refs, l_refs = stats[:n_pat], stats[n_pat:2 * n_pat], stats[2 * n_pat:]
    seq = q_ref.shape[0]
    head = pl.program_id(1)
    scale = 1.0 / math.sqrt(HEAD_DIM)
    blk = DSWA_BLK
    ii = lax.broadcasted_iota(jnp.int32, (blk, 2 * blk), 0)
    jj = lax.broadcasted_iota(jnp.int32, (blk, 2 * blk), 1)
    diff = ii + blk - jj
    band = (diff >= 0) & (diff <= blk)

    for p, (_, dil) in enumerate(DSWA_PATTERNS):
        nb = seq // dil // blk
        bucket = bucket_ref[p]
        bias = jnp.zeros((blk, 2 * blk), F32)
        for b in range(REL_BUCKETS):
            bias = jnp.where(bucket == b, rb_ref[head, b], bias)
        bias = jnp.where(band, bias, NEG_BIG)
        acc_ref, m_ref, l_ref = acc_refs[p], m_refs[p], l_refs[p]

        def rows(start, size, dil=dil):
            if dil == 1:
                return pl.ds(start, size)
            return pl.ds(start, size, stride=dil)

        def group(blocks, bias=bias, rows=rows, dil=dil,
                  acc_ref=acc_ref, m_ref=m_ref, l_ref=l_ref):
            rqs, rks, bss = [], [], []
            for r, n in blocks:
                rq = rows(n * blk * dil + r, blk)
                rqs.append(rq)
                rks.append(rq if n == 0 else rows((n - 1) * blk * dil + r, 2 * blk))
                bss.append(bias[:, blk:] if n == 0 else bias)
            qs = [(q_ref[rq, :] * scale).astype(BF16) for rq in rqs]
            ks = [k_ref[rk, :].astype(BF16) for rk in rks]
            ss = [lax.dot_general(q, k, (((1,), (1,)), ((), ())), preferred_element_type=F32) + bs
                  for q, k, bs in zip(qs, ks, bss)]
            ms = [jnp.max(s, axis=-1, keepdims=True) for s in ss]
            ps = [jnp.exp(s - m) for s, m in zip(ss, ms)]
            vs = [v_ref[rk, :].astype(BF16) for rk in rks]
            for rq, m, pexp, v in zip(rqs, ms, ps, vs):
                acc_ref[rq, :] = jnp.dot(pexp.astype(BF16), v, preferred_element_type=F32)
                m_ref[rq, :] = jnp.broadcast_to(m, (blk, HEAD_DIM))
                l_ref[rq, :] = jnp.broadcast_to(jnp.sum(pexp, axis=-1, keepdims=True),
                                                (blk, HEAD_DIM))

        per_phase = min(nb, DSWA_GROUP)
        phases = max(1, min(dil, DSWA_GROUP // per_phase))

        def trip(t, carry, nb=nb, per_phase=per_phase, phases=phases, group=group):
            for n0 in range(0, nb, per_phase):
                group([(t * phases + dr, n0 + dn) for dr in range(phases) for dn in range(per_phase)])
            return carry

        lax.fori_loop(0, dil // phases, trip, 0)

    chunk = min(256, seq)

    def finish(c, carry):
        rs = pl.ds(pl.multiple_of(c * chunk, chunk), chunk)
        ms = [m_ref[rs, :] for m_ref in m_refs]
        m_all = ms[0]
        for m in ms[1:]:
            m_all = jnp.maximum(m_all, m)
        num = jnp.zeros((chunk, HEAD_DIM), F32)
        den = jnp.zeros((chunk, HEAD_DIM), F32)
        for p in range(n_pat):
            w = jnp.exp(ms[p] - m_all)
            num = num + w * acc_refs[p][rs, :]
            den = den + w * l_refs[p][rs, :]
        o_ref[rs, :] = _rms(num / den, g_ref[...]).astype(o_ref.dtype)
        return carry

    lax.fori_loop(0, seq // chunk, finish, 0)


def _dswa(rel_bias_t, qkv3, buckets, gain, n_heads):
    bsz, seq, _ = qkv3.shape
    w = n_heads * HEAD_DIM
    return pl.pallas_call(
        _dswa_kernel,
        out_shape=jax.ShapeDtypeStruct((bsz, seq, w), BF16),
        grid=(bsz, n_heads),
        in_specs=[pl.BlockSpec(memory_space=pltpu.SMEM),
                  pl.BlockSpec((None, seq, HEAD_DIM), lambda b, h: (b, 0, h)),
                  pl.BlockSpec((None, seq, HEAD_DIM), lambda b, h: (b, 0, n_heads + h)),
                  pl.BlockSpec((None, seq, HEAD_DIM), lambda b, h: (b, 0, 2 * n_heads + h)),
                  pl.BlockSpec(buckets.shape, lambda b, h: (0, 0, 0)),
                  pl.BlockSpec((1, HEAD_DIM), lambda b, h: (0, h))],
        out_specs=pl.BlockSpec((None, seq, HEAD_DIM), lambda b, h: (b, 0, h)),
        scratch_shapes=[pltpu.VMEM((seq, HEAD_DIM), F32)] * (3 * len(DSWA_PATTERNS)),
        compiler_params=_cparams(("parallel", "parallel"), V7X_VMEM_LIMIT),
        name="dswa",
    )(rel_bias_t, qkv3, qkv3, qkv3, buckets, gain)


def _neg_abs(z):
    bits = lax.bitcast_convert_type(z, jnp.uint32) | jnp.uint32(0x80000000)
    return lax.bitcast_convert_type(bits, F32)


def _sb_kernel(q_ref, kb_ref, vb_ref, tri_ref, g_ref, o_ref):
    tq = q_ref.shape[0]
    nh = q_ref.shape[1] // HEAD_DIM
    iq = pl.program_id(2)

    q_all = (q_ref[...] * (LOG2E / math.sqrt(HEAD_DIM))).astype(BF16)
    rs = tq // SB_ROW_SPLIT
    ti = lax.broadcasted_iota(jnp.int32, (tq, tq), 0)
    si = lax.broadcasted_iota(jnp.int32, (tq, tq), 1)
    causal = si < ti
    chains = [(h, s) for h in range(nh) for s in range(SB_ROW_SPLIT)]
    cols = [slice(h * HEAD_DIM, (h + 1) * HEAD_DIM) for h, _ in chains]
    rws = [slice(s * rs, (s + 1) * rs) for _, s in chains]
    qs = [q_all[r, c] for r, c in zip(rws, cols)]
    masks = [causal[r, :] for r in rws]

    def visit(jk, state, masked):
        r0 = pl.multiple_of(jk * tq, tq)
        carries, accs = state
        ks = [kb_ref[pl.ds(r0, tq), c] for c in cols]
        vs = [vb_ref[pl.ds(r0, tq), c] for c in cols]
        zs = [lax.dot_general(q, k, (((1,), (1,)), ((), ())), preferred_element_type=F32)
              for q, k in zip(qs, ks)]
        his, los, logsig, rowsums = [], [], [], []
        for z, m in zip(zs, masks):
            sp = jnp.maximum(z, 0.0) + jnp.log(1.0 + jnp.exp2(_neg_abs(z))) * LOG2E
            if masked:
                sp = jnp.where(m, sp, 0.0)
            hi = sp.astype(BF16)
            his.append(hi)
            los.append((sp - hi.astype(F32)).astype(BF16))
            logsig.append(z - sp)
            rowsums.append(jnp.sum(sp, axis=-1, keepdims=True))
        within = [jnp.dot(jnp.concatenate([hi, lo], axis=1), tri_ref[...], preferred_element_type=F32)
                  for hi, lo in zip(his, los)]
        probs = [jnp.exp2(ls + wi + carry) for ls, wi, carry in zip(logsig, within, carries)]
        if masked:
            probs = [jnp.where(m, a, 0.0) for m, a in zip(masks, probs)]
        accs = tuple(acc + jnp.dot(a.astype(BF16), v, preferred_element_type=F32)
                     for acc, a, v in zip(accs, probs, vs))
        carries = tuple(carry - rsum for carry, rsum in zip(carries, rowsums))
        return carries, accs

    def remaining(carries):
        left = carries[0]
        for c in carries[1:]:
            left = jnp.maximum(left, c)
        return jnp.max(left)

    state = (tuple(jnp.zeros((rs, 1), F32) for _ in chains),
             tuple(jnp.zeros((rs, HEAD_DIM), F32) for _ in chains))
    state = visit(iq, state, True)

    def more(loop):
        t, left, _ = loop
        return (t < iq) & (left > SB_UNDERFLOW_LOG2)

    def step(loop):
        t, _, st = loop
        st = visit(iq - 1 - t, st, False)
        return t + 1, remaining(st[0]), st

    _, _, state = lax.while_loop(more, step, (jnp.int32(0), remaining(state[0]), state))
    for r, c, acc in zip(rws, cols, state[1]):
        o_ref[r, c] = _rms(acc, g_ref[:, c]).astype(o_ref.dtype)


def _sb(q_src, kv, tri2, gain, n_heads, q_col0):
    bsz, seq, _ = kv.shape
    tq = min(SB_TILE, seq)
    nh = min(SB_HEADS, n_heads)
    wblk = nh * HEAD_DIM
    groups = n_heads // nh
    c0 = q_col0 // nh
    return pl.pallas_call(
        _sb_kernel,
        out_shape=jax.ShapeDtypeStruct((bsz, seq, n_heads * HEAD_DIM), BF16),
        grid=(bsz, groups, seq // tq),
        in_specs=[pl.BlockSpec((None, tq, wblk), lambda b, h, i: (b, i, c0 + h)),
                  pl.BlockSpec((None, seq, wblk), lambda b, h, i: (b, 0, h)),
                  pl.BlockSpec((None, seq, wblk), lambda b, h, i: (b, 0, groups + h)),
                  pl.BlockSpec(tri2.shape, lambda b, h, i: (0, 0)),
                  pl.BlockSpec((1, wblk), lambda b, h, i: (0, h))],
        out_specs=pl.BlockSpec((None, tq, wblk), lambda b, h, i: (b, i, h)),
        compiler_params=_cparams(("parallel", "parallel", "arbitrary"), V7X_VMEM_LIMIT),
        name="sb",
    )(q_src, kv, kv, tri2, gain)


def _pack_halves(v):
    c = v.shape[1] // 2
    bits = lax.bitcast_convert_type(v, jnp.uint32)
    return (bits[:, :c] >> 16) | (bits[:, c:] & jnp.uint32(0xFFFF0000))


def _unpack_halves(p):
    lo = lax.bitcast_convert_type(p << 16, F32).astype(BF16)
    hi = lax.bitcast_convert_type(p & jnp.uint32(0xFFFF0000), F32).astype(BF16)
    return jnp.concatenate([lo, hi], axis=1)


def _outproj_kernel(oa_ref, ob_ref, w_ref, x_ref, mod_ref, gpa_ref, gpf_ref, wr_hi_ref, wr_lo_ref,
                    x1_ref, h2p_ref, h2b_ref, lg_ref):
    wa = oa_ref.shape[1]
    o = jnp.dot(oa_ref[...], w_ref[0:wa, :], preferred_element_type=F32)
    o = o + jnp.dot(ob_ref[...], w_ref[wa:, :], preferred_element_type=F32)
    x1 = x_ref[...] + mod_ref[2:3, :] * _rms(o, gpa_ref[...])
    x1_ref[...] = x1
    h2 = _rms(x1, gpf_ref[...]) * (1.0 + mod_ref[4:5, :]) + mod_ref[3:4, :]
    hi = h2.astype(BF16)
    h2b_ref[...] = hi
    hi_f = hi.astype(F32)
    h2p_ref[...] = _pack_halves(hi_f)
    lo = (h2 - hi_f).astype(BF16)
    lg = jnp.dot(hi, wr_hi_ref[...], preferred_element_type=F32)
    lg = lg + jnp.dot(lo, wr_hi_ref[...], preferred_element_type=F32)
    lg = lg + jnp.dot(hi, wr_lo_ref[...], preferred_element_type=F32)
    lg_ref[...] = lg


def _outproj(oa, ob, w_bf, x2, mod, gpa, gpf, wr_hi, wr_lo, seq):
    n, d = x2.shape
    wa = oa.shape[1]
    ne = wr_hi.shape[1]
    tm = min(256, seq)
    per_b = seq // tm
    const = lambda i: (0, 0)
    return pl.pallas_call(
        _outproj_kernel,
        out_shape=(jax.ShapeDtypeStruct((n, d), F32), jax.ShapeDtypeStruct((n, d // 2), jnp.uint32),
                   jax.ShapeDtypeStruct((n, d), BF16), jax.ShapeDtypeStruct((n, ne), F32)),
        grid=(n // tm,),
        in_specs=[pl.BlockSpec((tm, wa), lambda i: (i, 0)),
                  pl.BlockSpec((tm, wa), lambda i: (i, 0)),
                  pl.BlockSpec(w_bf.shape, const),
                  pl.BlockSpec((tm, d), lambda i: (i, 0)),
                  pl.BlockSpec((None, 6, d), lambda i: (i // per_b, 0, 0)),
                  pl.BlockSpec((1, d), const),
                  pl.BlockSpec((1, d), const),
                  pl.BlockSpec(wr_hi.shape, const),
                  pl.BlockSpec(wr_lo.shape, const)],
        out_specs=(pl.BlockSpec((tm, d), lambda i: (i, 0)),
                   pl.BlockSpec((tm, d // 2), lambda i: (i, 0)),
                   pl.BlockSpec((tm, d), lambda i: (i, 0)),
                   pl.BlockSpec((tm, ne), lambda i: (i, 0))),
        compiler_params=_cparams(("parallel",), V7X_VMEM_LIMIT),
        name="outproj",
    )(oa, ob, w_bf, x2, mod, gpa, gpf, wr_hi, wr_lo)


def _topk(lg, rbias):
    tr, ne = lg.shape
    scores = 1.0 / (1.0 + jnp.exp(-lg))
    sel = scores + rbias
    lane = lax.broadcasted_iota(jnp.int32, (tr, ne), 1).astype(F32)
    onehots, gates = [], []
    chosen = jnp.zeros((tr, ne), F32)
    for _ in range(TOP_K):
        m = jnp.max(sel, axis=-1, keepdims=True)
        idx = jnp.min(jnp.where(sel == m, lane, float(ne)), axis=-1, keepdims=True)
        oh = lane == idx
        gates.append(jnp.sum(jnp.where(oh, scores, 0.0), axis=-1, keepdims=True))
        sel = jnp.where(oh, -jnp.inf, sel)
        chosen = jnp.where(oh, 1.0, chosen)
        onehots.append(oh)
    return onehots, gates, chosen


def _route_count_kernel(lg_ref, rbias_ref, cnt_ref):
    @pl.when(pl.program_id(0) == 0)
    def _():
        cnt_ref[...] = jnp.zeros_like(cnt_ref)

    _, _, chosen = _topk(lg_ref[...], rbias_ref[...])
    cnt_ref[0:1, :] = cnt_ref[0:1, :] + jnp.sum(chosen, axis=0, keepdims=True)


def _route_count(logits, rbias, tr):
    n, ne = logits.shape
    return pl.pallas_call(
        _route_count_kernel,
        out_shape=jax.ShapeDtypeStruct((8, ne), F32),
        grid=(n // tr,),
        in_specs=[pl.BlockSpec((tr, ne), lambda i: (i, 0)),
                  pl.BlockSpec((1, ne), lambda i: (0, 0))],
        out_specs=pl.BlockSpec((8, ne), lambda i: (0, 0)),
        compiler_params=_cparams(("arbitrary",)),
        name="route_count",
    )(logits, rbias)


def _route_assign_kernel(lg_ref, rbias_ref, tri_ref, start_ref, dest_ref, gate_ref, seen_ref):
    tr = lg_ref.shape[0]

    @pl.when(pl.program_id(0) == 0)
    def _():
        seen_ref[...] = jnp.zeros_like(seen_ref)

    onehots, gates, chosen = _topk(lg_ref[...], rbias_ref[...])
    gsum = gates[0]
    for g in gates[1:]:
        gsum = gsum + g
    before = jnp.dot(tri_ref[...], chosen.astype(BF16), preferred_element_type=F32)
    slot = before + seen_ref[0:1, :] + start_ref[...]
    seen_ref[0:1, :] = seen_ref[0:1, :] + jnp.sum(chosen, axis=0, keepdims=True)

    out_lane = lax.broadcasted_iota(jnp.int32, (tr, OUT_LANES), 1)
    dest_o = jnp.zeros((tr, OUT_LANES), jnp.int32)
    gate_o = jnp.zeros((tr, OUT_LANES), F32)
    for k in range(TOP_K):
        dest = jnp.sum(jnp.where(onehots[k], slot, 0.0), axis=-1, keepdims=True)
        dest_o = jnp.where(out_lane == k, dest.astype(jnp.int32), dest_o)
        gate_o = jnp.where(out_lane == k, gates[k] / gsum * ROUTED_SCALING, gate_o)
    dest_ref[...] = dest_o
    gate_ref[...] = gate_o


def _route_assign(logits, rbias, tri, start):
    n, ne = logits.shape
    tr = tri.shape[0]
    return pl.pallas_call(
        _route_assign_kernel,
        out_shape=(jax.ShapeDtypeStruct((n, OUT_LANES), jnp.int32),
                   jax.ShapeDtypeStruct((n, OUT_LANES), F32)),
        grid=(n // tr,),
        in_specs=[pl.BlockSpec((tr, ne), lambda i: (i, 0)),
                  pl.BlockSpec((1, ne), lambda i: (0, 0)),
                  pl.BlockSpec((tr, tr), lambda i: (0, 0)),
                  pl.BlockSpec((1, ne), lambda i: (0, 0))],
        out_specs=(pl.BlockSpec((tr, OUT_LANES), lambda i: (i, 0)),
                   pl.BlockSpec((tr, OUT_LANES), lambda i: (i, 0))),
        scratch_shapes=[pltpu.VMEM((8, ne), F32)],
        compiler_params=_cparams(("arbitrary",)),
        name="route_assign",
    )(logits, rbias, tri, start)


def _dispatch_kernel(dest_ref, lo_ref, hi_ref, h_ref, xs_ref, zero_ref, sem, zsem):
    td = h_ref.shape[0]
    step = pl.program_id(0)
    n_exp = lo_ref.shape[0]
    tile = zero_ref.shape[0]

    @pl.when(step == 0)
    def _():
        zero_ref[...] = jnp.zeros_like(zero_ref)

        def zcopy(p):
            return pltpu.make_async_copy(zero_ref.at[pl.ds(0, 1)], xs_ref.at[pl.ds(p, 1)], zsem.at[0])

        def fill(e, c):
            lax.fori_loop(lo_ref[e], hi_ref[e], lambda p, c2: (zcopy(p).start(), c2)[1], 0)
            lax.fori_loop(lo_ref[e], hi_ref[e], lambda p, c2: (zcopy(p).wait(), c2)[1], 0)
            return c

        lax.fori_loop(0, n_exp, fill, 0)

        def tcopy(t):
            rows = pl.ds(pl.multiple_of(t * tile, tile), tile)
            return pltpu.make_async_copy(zero_ref, xs_ref.at[rows], zsem.at[1])

        first, last = hi_ref[n_exp - 1] // tile, xs_ref.shape[0] // tile
        lax.fori_loop(first, last, lambda t, c2: (tcopy(t).start(), c2)[1], 0)
        lax.fori_loop(first, last, lambda t, c2: (tcopy(t).wait(), c2)[1], 0)

    base = step * (td * TOP_K)

    def copy(t, k):
        return pltpu.make_async_copy(h_ref.at[pl.ds(t, 1)],
                                     xs_ref.at[pl.ds(dest_ref[base + t * TOP_K + k], 1)], sem.at[0])

    def start(t, c):
        for k in range(TOP_K):
            copy(t, k).start()
        return c

    def wait(t, c):
        for k in range(TOP_K):
            copy(t, k).wait()
        return c

    lax.fori_loop(0, td, start, 0)
    lax.fori_loop(0, td, wait, 0)


def _dispatch(dest_flat, fill_lo, fill_hi, h2, n_rows):
    n, d = h2.shape
    td = min(DISPATCH_TILE, n)
    return pl.pallas_call(
        _dispatch_kernel,
        out_shape=jax.ShapeDtypeStruct((n_rows, d), h2.dtype),
        grid_spec=pltpu.PrefetchScalarGridSpec(
            num_scalar_prefetch=3,
            grid=(n // td,),
            in_specs=[pl.BlockSpec((td, d), lambda i, *_: (i, 0))],
            out_specs=pl.BlockSpec(memory_space=pl.ANY),
            scratch_shapes=[pltpu.VMEM((MOE_TILE, d), h2.dtype),
                            pltpu.SemaphoreType.DMA((1,)),
                            pltpu.SemaphoreType.DMA((2,))]),
        compiler_params=_cparams(("arbitrary",)),
        name="dispatch",
    )(dest_flat, fill_lo, fill_hi, h2)


def _moe_kernel(te_ref, nu_ref, slot_ref, nxt_ref, ulo_ref, uhi_ref,
                x_ref, wg_hbm, wu_hbm, wd_hbm, y_ref,
                wg_bf, wu_bf, wd_bf, stage_gu, stage_d, sem):
    i = pl.program_id(0)
    d, f = wg_bf.shape[1], wg_bf.shape[2]
    rg, rd = stage_gu.shape[1], stage_d.shape[1]
    ng, nd = d // rg, f // rd
    n_units = 2 * ng + nd
    kinds = ((0, ng, wg_hbm, wg_bf, stage_gu, rg), (ng, 2 * ng, wu_hbm, wu_bf, stage_gu, rg),
             (2 * ng, n_units, wd_hbm, wd_bf, stage_d, rd))

    def chunk_copy(kind, u, e):
        first, _, src, _, stage, rows = kind
        r0 = pl.multiple_of((u - first) * rows, rows)
        return pltpu.make_async_copy(src.at[e, pl.ds(r0, rows), :], stage.at[u % 2], sem.at[u % 2])

    def request(u, e):
        for kind in kinds:
            @pl.when((u >= kind[0]) & (u < kind[1]))
            def _(kind=kind):
                chunk_copy(kind, u, e).start()

    def receive(u, e, slot):
        for kind in kinds:
            @pl.when((u >= kind[0]) & (u < kind[1]))
            def _(kind=kind):
                first, _, _, dst, stage, rows = kind
                chunk_copy(kind, u, e).wait()
                r0 = pl.multiple_of((u - first) * rows, rows)
                dst[slot, pl.ds(r0, rows), :] = stage[u % 2].astype(BF16)

    def run_units(lo, hi, e, slot):
        def body(u, c):
            @pl.when(u == 0)
            def _():
                request(u, e)

            @pl.when(u + 1 < n_units)
            def _():
                request(u + 1, e)

            receive(u, e, slot)
            return c

        lax.fori_loop(lo, hi, body, 0)

    slot = slot_ref[i]

    @pl.when(i == 0)
    def _():
        run_units(0, n_units, te_ref[0], slot)

    run_units(ulo_ref[i], uhi_ref[i], nxt_ref[i], 1 - slot)

    @pl.when(i < nu_ref[0])
    def _():
        x = _unpack_halves(x_ref[...])
        hg = jnp.dot(x, wg_bf[slot], preferred_element_type=F32)
        hu = jnp.dot(x, wu_bf[slot], preferred_element_type=F32)
        h = (hg / (1.0 + jnp.exp(-hg))) * hu
        y_ref[...] = jnp.dot(h.astype(BF16), wd_bf[slot], preferred_element_type=F32)

    @pl.when(i >= nu_ref[0])
    def _():
        y_ref[...] = jnp.zeros_like(y_ref)


def _moe(tile_expert, n_used, tile_slot, tile_next, unit_lo, unit_hi, xs, wg, wu, wd):
    rows, dp = xs.shape
    _, d, f = wg.shape
    tm = MOE_TILE
    rg, rd = d // MOE_GU_CHUNKS, f // MOE_DOWN_CHUNKS
    return pl.pallas_call(
        _moe_kernel,
        out_shape=jax.ShapeDtypeStruct((rows, d), F32),
        grid_spec=pltpu.PrefetchScalarGridSpec(
            num_scalar_prefetch=6,
            grid=(rows // tm,),
            in_specs=[pl.BlockSpec((tm, dp), lambda i, te, nu, *_: (jnp.minimum(i, nu[0] - 1), 0)),
                      pl.BlockSpec(memory_space=pl.ANY),
                      pl.BlockSpec(memory_space=pl.ANY),
                      pl.BlockSpec(memory_space=pl.ANY)],
            out_specs=pl.BlockSpec((tm, d), lambda i, *_: (i, 0)),
            scratch_shapes=[pltpu.VMEM((2, d, f), BF16),
                            pltpu.VMEM((2, d, f), BF16),
                            pltpu.VMEM((2, f, d), BF16),
                            pltpu.VMEM((2, rg, f), F32),
                            pltpu.VMEM((2, rd, d), F32),
                            pltpu.SemaphoreType.DMA((2,))]),
        compiler_params=_cparams(("arbitrary",), V7X_VMEM_LIMIT_MOE),
        name="moe",
    )(tile_expert, n_used, tile_slot, tile_next, unit_lo, unit_hi, xs, wg, wu, wd)


def _shared_kernel(x_ref, wg_ref, wu_ref, wd_ref, y_ref):
    x = x_ref[...]
    hg = jnp.dot(x, wg_ref[...], preferred_element_type=F32)
    hu = jnp.dot(x, wu_ref[...], preferred_element_type=F32)
    h = (hg / (1.0 + jnp.exp(-hg))) * hu
    y_ref[...] = jnp.dot(h.astype(BF16), wd_ref[...], preferred_element_type=F32)


def _shared(h2b, wg, wu, wd):
    n, d = h2b.shape
    f = wg.shape[1]
    tm = min(512, n)
    const = lambda i: (0, 0)
    return pl.pallas_call(
        _shared_kernel,
        out_shape=jax.ShapeDtypeStruct((n, d), F32),
        grid=(n // tm,),
        in_specs=[pl.BlockSpec((tm, d), lambda i: (i, 0)),
                  pl.BlockSpec((d, f), const),
                  pl.BlockSpec((d, f), const),
                  pl.BlockSpec((f, d), const)],
        out_specs=pl.BlockSpec((tm, d), lambda i: (i, 0)),
        compiler_params=_cparams(("parallel",), V7X_VMEM_LIMIT),
        name="shared",
    )(h2b, wg, wu, wd)


def _combine_kernel(dest_ref, y_ref, gate_ref, sh_ref, x1_ref, mod_ref, g_ref, o_ref, buf_ref, sem):
    tc = x1_ref.shape[0]
    step = pl.program_id(0)
    n_steps = pl.num_programs(0)

    def copy(tile, slot, t, k):
        src = dest_ref[(tile * tc + t) * TOP_K + k]
        return pltpu.make_async_copy(y_ref.at[pl.ds(src, 1)], buf_ref.at[slot, k, pl.ds(t, 1)],
                                     sem.at[slot])

    def request(tile, slot):
        def body(t, c):
            for k in range(TOP_K):
                copy(tile, slot, t, k).start()
            return c
        lax.fori_loop(0, tc, body, 0)

    slot = step % 2

    @pl.when(step == 0)
    def _():
        request(step, slot)

    @pl.when(step + 1 < n_steps)
    def _():
        request(step + 1, 1 - slot)

    def await_(t, c):
        for k in range(TOP_K):
            copy(step, slot, t, k).wait()
        return c

    lax.fori_loop(0, tc, await_, 0)

    gate = gate_ref[...]
    y = sh_ref[...]
    for k in range(TOP_K):
        y = y + gate[:, k:k + 1] * buf_ref[slot, k]
    o_ref[...] = x1_ref[...] + mod_ref[5:6, :] * _rms(y, g_ref[...])


def _combine(dest_flat, y, gates, shared, x1, mod, g, seq):
    n, d = x1.shape
    tc = min(COMBINE_TILE, seq)
    per_b = seq // tc
    return pl.pallas_call(
        _combine_kernel,
        out_shape=jax.ShapeDtypeStruct((n, d), F32),
        grid_spec=pltpu.PrefetchScalarGridSpec(
            num_scalar_prefetch=1,
            grid=(n // tc,),
            in_specs=[pl.BlockSpec(memory_space=pl.ANY),
                      pl.BlockSpec((tc, OUT_LANES), lambda i, dest: (i, 0)),
                      pl.BlockSpec((tc, d), lambda i, dest: (i, 0)),
                      pl.BlockSpec((tc, d), lambda i, dest: (i, 0)),
                      pl.BlockSpec((None, 6, d), lambda i, dest: (i // per_b, 0, 0)),
                      pl.BlockSpec((1, d), lambda i, dest: (0, 0))],
            out_specs=pl.BlockSpec((tc, d), lambda i, dest: (i, 0)),
            scratch_shapes=[pltpu.VMEM((2, TOP_K, tc, d), F32),
                            pltpu.SemaphoreType.DMA((2,))]),
        compiler_params=_cparams(("arbitrary",), V7X_VMEM_LIMIT),
        name="combine",
    )(dest_flat, y, gates, shared, x1, mod, g)


def _t5_bucket(dist):
    max_exact = REL_BUCKETS // 2
    d_f = jnp.maximum(dist, max_exact).astype(F32)
    large = max_exact + (jnp.log(d_f / max_exact) / math.log(REL_MAX_DISTANCE / max_exact)
                         * (REL_BUCKETS - max_exact)).astype(jnp.int32)
    large = jnp.minimum(large, REL_BUCKETS - 1)
    return jnp.where(dist < max_exact, dist, large)


def _bucket_tables():
    i = jnp.arange(DSWA_BLK)[:, None]
    j = jnp.arange(2 * DSWA_BLK)[None, :]
    diff = jnp.maximum(i + DSWA_BLK - j, 0)
    return jnp.stack([_t5_bucket(diff * dil) for _, dil in DSWA_PATTERNS]).astype(jnp.int32)


def _strict_lower(n):
    return jnp.arange(n)[:, None] > jnp.arange(n)[None, :]


def _layer(x, c_pad, rel_bias, norm_pre_attn, norm_post_attn, norm_pre_ffn, norm_post_ffn,
           w_ada, b_ada, w_in, head_gain_dswa, head_gain_sb, w_out, w_router, router_bias,
           w_gate_e, w_up_e, w_down_e, w_gate_s, w_up_s, w_down_s):
    bsz, seq, d = x.shape
    n = bsz * seq
    n_heads = d // (2 * HEAD_DIM)
    n_exp = w_router.shape[1]
    row = lambda v: v.reshape(1, -1)

    mod = _ada(c_pad, w_ada, row(b_ada))[:bsz].reshape(bsz, 6, d)
    x2 = x.reshape(n, d)

    w_in_bf = w_in.astype(BF16)
    qkvq = _qkv(x2, mod, row(norm_pre_attn), w_in_bf, seq, 0, 2 * d, F32).reshape(bsz, seq, 2 * d)
    kv_sb = _qkv(x2, mod, row(norm_pre_attn), w_in_bf, seq, 2 * d, 3 * d, BF16).reshape(bsz, seq, d)

    oa = _dswa(rel_bias.T, qkvq, _bucket_tables(), row(head_gain_dswa), n_heads)
    tq = min(SB_TILE, seq)
    neg_tri = jnp.where(_strict_lower(tq), -1.0, 0.0).astype(BF16)
    ob = _sb(qkvq, kv_sb, jnp.concatenate([neg_tri, neg_tri], axis=0), row(head_gain_sb), n_heads,
             3 * n_heads)

    wr_hi = w_router.astype(BF16)
    wr_lo = (w_router - wr_hi.astype(F32)).astype(BF16)
    x1, h2p, h2b, logits = _outproj(oa.reshape(n, -1), ob.reshape(n, -1), w_out.astype(BF16), x2, mod,
                                   row(norm_post_attn), row(norm_pre_ffn), wr_hi, wr_lo, seq)

    tr = min(ROUTE_TILE, n)
    counts = _route_count(logits, row(router_bias), tr)[0].astype(jnp.int32)
    padded = (counts + MOE_TILE - 1) // MOE_TILE * MOE_TILE
    pad_end = jnp.cumsum(padded)
    pad_start = pad_end - padded
    n_rows = (n * TOP_K + n_exp * (MOE_TILE - 1) + MOE_TILE - 1) // MOE_TILE * MOE_TILE
    n_tiles = n_rows // MOE_TILE
    n_used = jnp.maximum(pad_end[-1] // MOE_TILE, 1).astype(jnp.int32)
    tile_id = jnp.minimum(jnp.arange(n_tiles, dtype=jnp.int32), n_used - 1)
    tile_expert = jnp.minimum(
        jnp.sum(pad_end[None, :] <= (tile_id * MOE_TILE)[:, None], axis=1), n_exp - 1).astype(jnp.int32)

    n_units = 2 * MOE_GU_CHUNKS + MOE_DOWN_CHUNKS
    nonempty = padded > 0
    ids = jnp.arange(n_exp, dtype=jnp.int32)
    later = lax.cummin(jnp.where(nonempty, ids, n_exp)[::-1])[::-1]
    next_e = jnp.concatenate([later[1:], jnp.full((1,), n_exp, jnp.int32)])
    order = jnp.cumsum(nonempty.astype(jnp.int32)) - 1
    t_in = tile_id - (pad_start // MOE_TILE)[tile_expert]
    t_cnt = jnp.maximum((padded // MOE_TILE)[tile_expert], 1)
    live = (jnp.arange(n_tiles) < n_used) & (next_e[tile_expert] < n_exp)
    unit_lo = jnp.where(live, n_units * t_in // t_cnt, 0).astype(jnp.int32)
    unit_hi = jnp.where(live, n_units * (t_in + 1) // t_cnt, 0).astype(jnp.int32)
    tile_next = jnp.minimum(next_e[tile_expert], n_exp - 1).astype(jnp.int32)
    tile_slot = (order[tile_expert] % 2).astype(jnp.int32)

    dest, gates = _route_assign(logits, row(router_bias), _strict_lower(tr).astype(BF16),
                                row(pad_start.astype(F32)))
    dest_flat = dest[:, :TOP_K].reshape(-1)

    xs = _dispatch(dest_flat, (pad_start + counts).astype(jnp.int32), pad_end.astype(jnp.int32), h2p, n_rows)
    y = _moe(tile_expert, n_used.reshape(1), tile_slot, tile_next, unit_lo, unit_hi, xs,
             w_gate_e, w_up_e, w_down_e)
    shared = _shared(h2b, w_gate_s.astype(BF16), w_up_s.astype(BF16), w_down_s.astype(BF16))
    out = _combine(dest_flat, y, gates, shared, x1, mod, row(norm_post_ffn), seq)
    return out.reshape(bsz, seq, d)


def kernel(x, c, rel_bias, norm_pre_attn, norm_post_attn, norm_pre_ffn, norm_post_ffn, w_ada, b_ada, w_in, head_gain_dswa, head_gain_sb, w_out, w_router, router_bias, w_gate_e, w_up_e, w_down_e, w_gate_s, w_up_s, w_down_s):
    bsz = x.shape[0]
    depth = w_ada.shape[0]
    c_pad = jnp.zeros((8, c.shape[1]), F32).at[:bsz].set(c)
    for l in range(depth):
        x = _layer(x, c_pad, rel_bias, norm_pre_attn[l], norm_post_attn[l], norm_pre_ffn[l],
                   norm_post_ffn[l], w_ada[l], b_ada[l], w_in[l], head_gain_dswa[l], head_gain_sb[l],
                   w_out[l], w_router[l], router_bias[l], w_gate_e[l], w_up_e[l], w_down_e[l],
                   w_gate_s[l], w_up_s[l], w_down_s[l])
    return x
```

```python
import math

import jax
import jax.numpy as jnp
from jax import lax
from jax.experimental import pallas as pl
from jax.experimental.pallas import tpu as pltpu

F32 = jnp.float32
BF16 = jnp.bfloat16

HEAD_DIM = 128
DSWA_PATTERNS = ((128, 1), (512, 4), (2048, 16))
DSWA_BLK = 128
DSWA_GROUP = 4
REL_BUCKETS = 32
REL_MAX_DISTANCE = 2048
TOP_K = 6
ROUTED_SCALING = 2.446
NORM_EPS = 1e-6
NEG_BIG = -1e30
LOG2E = 1.4426950408889634

V7X_VMEM_LIMIT = 56 * 1024 * 1024
V7X_VMEM_LIMIT_MOE = 60 * 1024 * 1024
MOE_TILE = 256
MOE_CHUNK_ROWS = 128
MOE_RING = 8
SB_TILE = 256
SB_HEADS = 4
SB_ROW_SPLIT = 2
SB_UNDERFLOW_LOG2 = -160.0
ROUTE_TILE = 512
DISPATCH_TILE = 256
COMBINE_TILE = 128
OUT_LANES = 128


def _cparams(sem, vmem=None):
    return pltpu.CompilerParams(dimension_semantics=sem, vmem_limit_bytes=vmem)


def _rms(x, g):
    return x * lax.rsqrt(jnp.mean(x * x, axis=-1, keepdims=True) + NORM_EPS) * g


def _ada_kernel(c_ref, w_ref, b_ref, o_ref):
    c = c_ref[...]
    ca = c / (1.0 + jnp.exp(-c))
    o_ref[...] = jnp.dot(ca.astype(BF16), w_ref[...].astype(BF16),
                         preferred_element_type=F32) + b_ref[...]


def _ada(c_pad, w_ada, b_ada):
    rows, d = c_pad.shape
    n = w_ada.shape[1]
    tn = min(n, 1536)
    return pl.pallas_call(
        _ada_kernel,
        out_shape=jax.ShapeDtypeStruct((rows, n), F32),
        grid=(n // tn,),
        in_specs=[pl.BlockSpec((rows, d), lambda j: (0, 0)),
                  pl.BlockSpec((d, tn), lambda j: (0, j)),
                  pl.BlockSpec((1, tn), lambda j: (0, j))],
        out_specs=pl.BlockSpec((rows, tn), lambda j: (0, j)),
        compiler_params=_cparams(("arbitrary",), V7X_VMEM_LIMIT),
        name="ada",
    )(c_pad, w_ada, b_ada)


def _qkv_kernel(x_ref, mod_ref, g_ref, w_ref, o_ref, h_ref):
    @pl.when(pl.program_id(1) == 0)
    def _():
        h = _rms(x_ref[...], g_ref[...]) * (1.0 + mod_ref[1:2, :]) + mod_ref[0:1, :]
        h_ref[...] = h.astype(BF16)

    o_ref[...] = jnp.dot(h_ref[...], w_ref[...], preferred_element_type=F32).astype(o_ref.dtype)


def _qkv(x2, mod, g, w_bf, seq, col_lo, col_hi, out_dtype):
    n, d = x2.shape
    nout = col_hi - col_lo
    tm = min(1024, seq)
    tn = min(1024, nout)
    j0 = col_lo // tn
    per_b = seq // tm
    return pl.pallas_call(
        _qkv_kernel,
        out_shape=jax.ShapeDtypeStruct((n, nout), out_dtype),
        grid=(n // tm, nout // tn),
        in_specs=[pl.BlockSpec((tm, d), lambda i, j: (i, 0)),
                  pl.BlockSpec((None, 6, d), lambda i, j: (i // per_b, 0, 0)),
                  pl.BlockSpec((1, d), lambda i, j: (0, 0)),
                  pl.BlockSpec((d, tn), lambda i, j: (0, j0 + j))],
        out_specs=pl.BlockSpec((tm, tn), lambda i, j: (i, j)),
        scratch_shapes=[pltpu.VMEM((tm, d), BF16)],
        compiler_params=_cparams(("parallel", "arbitrary"), V7X_VMEM_LIMIT),
        name="qkv",
    )(x2, mod, g, w_bf)


def _dswa_kernel(rb_ref, q_ref, k_ref, v_ref, bucket_ref, g_ref, o_ref, *stats):
    n_pat = len(DSWA_PATTERNS)
    acc_refs, m_refs, l_refs = stats[:n_pat], stats[n_pat:2 * n_pat], stats[2 * n_pat:]
    seq = q_ref.shape[0]
    head = pl.program_id(1)
    scale = 1.0 / math.sqrt(HEAD_DIM)
    blk = DSWA_BLK
    ii = lax.broadcasted_iota(jnp.int32, (blk, 2 * blk), 0)
    jj = lax.broadcasted_iota(jnp.int32, (blk, 2 * blk), 1)
    diff = ii + blk - jj
    band = (diff >= 0) & (diff <= blk)

    for p, (_, dil) in enumerate(DSWA_PATTERNS):
        nb = seq // dil // blk
        bucket = bucket_ref[p]
        bias = jnp.zeros((blk, 2 * blk), F32)
        for b in range(REL_BUCKETS):
            bias = jnp.where(bucket == b, rb_ref[head, b], bias)
        bias = jnp.where(band, bias, NEG_BIG)
        acc_ref, m_ref, l_ref = acc_refs[p], m_refs[p], l_refs[p]

        def rows(start, size, dil=dil):
            if dil == 1:
                return pl.ds(start, size)
            return pl.ds(start, size, stride=dil)

        def group(blocks, bias=bias, rows=rows, dil=dil,
                  acc_ref=acc_ref, m_ref=m_ref, l_ref=l_ref):
            rqs, rks, bss = [], [], []
            for r, n in blocks:
                rq = rows(n * blk * dil + r, blk)
                rqs.append(rq)
                rks.append(rq if n == 0 else rows((n - 1) * blk * dil + r, 2 * blk))
                bss.append(bias[:, blk:] if n == 0 else bias)
            qs = [(q_ref[rq, :] * scale).astype(BF16) for rq in rqs]
            ks = [k_ref[rk, :].astype(BF16) for rk in rks]
            ss = [lax.dot_general(q, k, (((1,), (1,)), ((), ())), preferred_element_type=F32) + bs
                  for q, k, bs in zip(qs, ks, bss)]
            ms = [jnp.max(s, axis=-1, keepdims=True) for s in ss]
            ps = [jnp.exp(s - m) for s, m in zip(ss, ms)]
            vs = [v_ref[rk, :].astype(BF16) for rk in rks]
            for rq, m, pexp, v in zip(rqs, ms, ps, vs):
                acc_ref[rq, :] = jnp.dot(pexp.astype(BF16), v, preferred_element_type=F32)
                m_ref[rq, :] = jnp.broadcast_to(m, (blk, HEAD_DIM))
                l_ref[rq, :] = jnp.broadcast_to(jnp.sum(pexp, axis=-1, keepdims=True),
                                                (blk, HEAD_DIM))

        per_phase = min(nb, DSWA_GROUP)
        phases = max(1, min(dil, DSWA_GROUP // per_phase))

        def trip(t, carry, nb=nb, per_phase=per_phase, phases=phases, group=group):
            for n0 in range(0, nb, per_phase):
                group([(t * phases + dr, n0 + dn) for dr in range(phases) for dn in range(per_phase)])
            return carry

        lax.fori_loop(0, dil // phases, trip, 0)

    chunk = min(256, seq)

    def finish(c, carry):
        rs = pl.ds(pl.multiple_of(c * chunk, chunk), chunk)
        ms = [m_ref[rs, :] for m_ref in m_refs]
        m_all = ms[0]
        for m in ms[1:]:
            m_all = jnp.maximum(m_all, m)
        num = jnp.zeros((chunk, HEAD_DIM), F32)
        den = jnp.zeros((chunk, HEAD_DIM), F32)
        for p in range(n_pat):
            w = jnp.exp(ms[p] - m_all)
            num = num + w * acc_refs[p][rs, :]
            den = den + w * l_refs[p][rs, :]
        o_ref[rs, :] = _rms(num / den, g_ref[...]).astype(o_ref.dtype)
        return carry

    lax.fori_loop(0, seq // chunk, finish, 0)


def _dswa(rel_bias_t, qkv3, buckets, gain, n_heads):
    bsz, seq, _ = qkv3.shape
    w = n_heads * HEAD_DIM
    return pl.pallas_call(
        _dswa_kernel,
        out_shape=jax.ShapeDtypeStruct((bsz, seq, w), BF16),
        grid=(bsz, n_heads),
        in_specs=[pl.BlockSpec(memory_space=pltpu.SMEM),
                  pl.BlockSpec((None, seq, HEAD_DIM), lambda b, h: (b, 0, h)),
                  pl.BlockSpec((None, seq, HEAD_DIM), lambda b, h: (b, 0, n_heads + h)),
                  pl.BlockSpec((None, seq, HEAD_DIM), lambda b, h: (b, 0, 2 * n_heads + h)),
                  pl.BlockSpec(buckets.shape, lambda b, h: (0, 0, 0)),
                  pl.BlockSpec((1, HEAD_DIM), lambda b, h: (0, h))],
        out_specs=pl.BlockSpec((None, seq, HEAD_DIM), lambda b, h: (b, 0, h)),
        scratch_shapes=[pltpu.VMEM((seq, HEAD_DIM), F32)] * (3 * len(DSWA_PATTERNS)),
        compiler_params=_cparams(("parallel", "parallel"), V7X_VMEM_LIMIT),
        name="dswa",
    )(rel_bias_t, qkv3, qkv3, qkv3, buckets, gain)


def _neg_abs(z):
    bits = lax.bitcast_convert_type(z, jnp.uint32) | jnp.uint32(0x80000000)
    return lax.bitcast_convert_type(bits, F32)


def _sb_kernel(q_ref, kb_ref, vb_ref, tri_ref, g_ref, o_ref):
    tq = q_ref.shape[0]
    nh = q_ref.shape[1] // HEAD_DIM
    iq = pl.program_id(2)

    q_all = (q_ref[...] * (LOG2E / math.sqrt(HEAD_DIM))).astype(BF16)
    rs = tq // SB_ROW_SPLIT
    ti = lax.broadcasted_iota(jnp.int32, (tq, tq), 0)
    si = lax.broadcasted_iota(jnp.int32, (tq, tq), 1)
    causal = si < ti
    chains = [(h, s) for h in range(nh) for s in range(SB_ROW_SPLIT)]
    cols = [slice(h * HEAD_DIM, (h + 1) * HEAD_DIM) for h, _ in chains]
    rws = [slice(s * rs, (s + 1) * rs) for _, s in chains]
    qs = [q_all[r, c] for r, c in zip(rws, cols)]
    masks = [causal[r, :] for r in rws]

    def visit(jk, state, masked):
        r0 = pl.multiple_of(jk * tq, tq)
        carries, accs = state
        ks = [kb_ref[pl.ds(r0, tq), c] for c in cols]
        vs = [vb_ref[pl.ds(r0, tq), c] for c in cols]
        zs = [lax.dot_general(q, k, (((1,), (1,)), ((), ())), preferred_element_type=F32)
              for q, k in zip(qs, ks)]
        his, los, logsig, rowsums = [], [], [], []
        for z, m in zip(zs, masks):
            sp = jnp.maximum(z, 0.0) + jnp.log(1.0 + jnp.exp2(_neg_abs(z))) * LOG2E
            if masked:
                sp = jnp.where(m, sp, 0.0)
            hi = sp.astype(BF16)
            his.append(hi)
            los.append((sp - hi.astype(F32)).astype(BF16))
            logsig.append(z - sp)
            rowsums.append(jnp.sum(sp, axis=-1, keepdims=True))
        within = [jnp.dot(jnp.concatenate([hi, lo], axis=1), tri_ref[...], preferred_element_type=F32)
                  for hi, lo in zip(his, los)]
        probs = [jnp.exp2(ls + wi + carry) for ls, wi, carry in zip(logsig, within, carries)]
        if masked:
            probs = [jnp.where(m, a, 0.0) for m, a in zip(masks, probs)]
        accs = tuple(acc + jnp.dot(a.astype(BF16), v, preferred_element_type=F32)
                     for acc, a, v in zip(accs, probs, vs))
        carries = tuple(carry - rsum for carry, rsum in zip(carries, rowsums))
        return carries, accs

    def remaining(carries):
        left = carries[0]
        for c in carries[1:]:
            left = jnp.maximum(left, c)
        return jnp.max(left)

    state = (tuple(jnp.zeros((rs, 1), F32) for _ in chains),
             tuple(jnp.zeros((rs, HEAD_DIM), F32) for _ in chains))
    state = visit(iq, state, True)

    def more(loop):
        t, left, _ = loop
        return (t < iq) & (left > SB_UNDERFLOW_LOG2)

    def step(loop):
        t, _, st = loop
        st = visit(iq - 1 - t, st, False)
        return t + 1, remaining(st[0]), st

    _, _, state = lax.while_loop(more, step, (jnp.int32(0), remaining(state[0]), state))
    for r, c, acc in zip(rws, cols, state[1]):
        o_ref[r, c] = _rms(acc, g_ref[:, c]).astype(o_ref.dtype)


def _sb(q_src, kv, tri2, gain, n_heads, q_col0):
    bsz, seq, _ = kv.shape
    tq = min(SB_TILE, seq)
    nh = min(SB_HEADS, n_heads)
    wblk = nh * HEAD_DIM
    groups = n_heads // nh
    c0 = q_col0 // nh
    return pl.pallas_call(
        _sb_kernel,
        out_shape=jax.ShapeDtypeStruct((bsz, seq, n_heads * HEAD_DIM), BF16),
        grid=(bsz, groups, seq // tq),
        in_specs=[pl.BlockSpec((None, tq, wblk), lambda b, h, i: (b, i, c0 + h)),
                  pl.BlockSpec((None, seq, wblk), lambda b, h, i: (b, 0, h)),
                  pl.BlockSpec((None, seq, wblk), lambda b, h, i: (b, 0, groups + h)),
                  pl.BlockSpec(tri2.shape, lambda b, h, i: (0, 0)),
                  pl.BlockSpec((1, wblk), lambda b, h, i: (0, h))],
        out_specs=pl.BlockSpec((None, tq, wblk), lambda b, h, i: (b, i, h)),
        compiler_params=_cparams(("parallel", "parallel", "arbitrary"), V7X_VMEM_LIMIT),
        name="sb",
    )(q_src, kv, kv, tri2, gain)


def _pack_halves(v):
    c = v.shape[1] // 2
    bits = lax.bitcast_convert_type(v, jnp.uint32)
    return (bits[:, :c] >> 16) | (bits[:, c:] & jnp.uint32(0xFFFF0000))


def _unpack_halves(p):
    lo = lax.bitcast_convert_type(p << 16, F32).astype(BF16)
    hi = lax.bitcast_convert_type(p & jnp.uint32(0xFFFF0000), F32).astype(BF16)
    return jnp.concatenate([lo, hi], axis=1)


def _outproj_kernel(oa_ref, ob_ref, w_ref, x_ref, mod_ref, gpa_ref, gpf_ref, wr_hi_ref, wr_lo_ref,
                    x1_ref, h2p_ref, h2b_ref, lg_ref):
    wa = oa_ref.shape[1]
    o = jnp.dot(oa_ref[...], w_ref[0:wa, :], preferred_element_type=F32)
    o = o + jnp.dot(ob_ref[...], w_ref[wa:, :], preferred_element_type=F32)
    x1 = x_ref[...] + mod_ref[2:3, :] * _rms(o, gpa_ref[...])
    x1_ref[...] = x1
    h2 = _rms(x1, gpf_ref[...]) * (1.0 + mod_ref[4:5, :]) + mod_ref[3:4, :]
    hi = h2.astype(BF16)
    h2b_ref[...] = hi
    hi_f = hi.astype(F32)
    h2p_ref[...] = _pack_halves(hi_f)
    lo = (h2 - hi_f).astype(BF16)
    lg = jnp.dot(hi, wr_hi_ref[...], preferred_element_type=F32)
    lg = lg + jnp.dot(lo, wr_hi_ref[...], preferred_element_type=F32)
    lg = lg + jnp.dot(hi, wr_lo_ref[...], preferred_element_type=F32)
    lg_ref[...] = lg


def _outproj(oa, ob, w_bf, x2, mod, gpa, gpf, wr_hi, wr_lo, seq):
    n, d = x2.shape
    wa = oa.shape[1]
    ne = wr_hi.shape[1]
    tm = min(256, seq)
    per_b = seq // tm
    const = lambda i: (0, 0)
    return pl.pallas_call(
        _outproj_kernel,
        out_shape=(jax.ShapeDtypeStruct((n, d), F32), jax.ShapeDtypeStruct((n, d // 2), jnp.uint32),
                   jax.ShapeDtypeStruct((n, d), BF16), jax.ShapeDtypeStruct((n, ne), F32)),
        grid=(n // tm,),
        in_specs=[pl.BlockSpec((tm, wa), lambda i: (i, 0)),
                  pl.BlockSpec((tm, wa), lambda i: (i, 0)),
                  pl.BlockSpec(w_bf.shape, const),
                  pl.BlockSpec((tm, d), lambda i: (i, 0)),
                  pl.BlockSpec((None, 6, d), lambda i: (i // per_b, 0, 0)),
                  pl.BlockSpec((1, d), const),
                  pl.BlockSpec((1, d), const),
                  pl.BlockSpec(wr_hi.shape, const),
                  pl.BlockSpec(wr_lo.shape, const)],
        out_specs=(pl.BlockSpec((tm, d), lambda i: (i, 0)),
                   pl.BlockSpec((tm, d // 2), lambda i: (i, 0)),
                   pl.BlockSpec((tm, d), lambda i: (i, 0)),
                   pl.BlockSpec((tm, ne), lambda i: (i, 0))),
        compiler_params=_cparams(("parallel",), V7X_VMEM_LIMIT),
        name="outproj",
    )(oa, ob, w_bf, x2, mod, gpa, gpf, wr_hi, wr_lo)


def _topk(lg, rbias):
    tr, ne = lg.shape
    scores = 1.0 / (1.0 + jnp.exp(-lg))
    sel = scores + rbias
    lane = lax.broadcasted_iota(jnp.int32, (tr, ne), 1).astype(F32)
    onehots, gates = [], []
    chosen = jnp.zeros((tr, ne), F32)
    for _ in range(TOP_K):
        m = jnp.max(sel, axis=-1, keepdims=True)
        idx = jnp.min(jnp.where(sel == m, lane, float(ne)), axis=-1, keepdims=True)
        oh = lane == idx
        gates.append(jnp.sum(jnp.where(oh, scores, 0.0), axis=-1, keepdims=True))
        sel = jnp.where(oh, -jnp.inf, sel)
        chosen = jnp.where(oh, 1.0, chosen)
        onehots.append(oh)
    return onehots, gates, chosen


def _route_count_kernel(lg_ref, rbias_ref, cnt_ref):
    @pl.when(pl.program_id(0) == 0)
    def _():
        cnt_ref[...] = jnp.zeros_like(cnt_ref)

    _, _, chosen = _topk(lg_ref[...], rbias_ref[...])
    cnt_ref[0:1, :] = cnt_ref[0:1, :] + jnp.sum(chosen, axis=0, keepdims=True)


def _route_count(logits, rbias, tr):
    n, ne = logits.shape
    return pl.pallas_call(
        _route_count_kernel,
        out_shape=jax.ShapeDtypeStruct((8, ne), F32),
        grid=(n // tr,),
        in_specs=[pl.BlockSpec((tr, ne), lambda i: (i, 0)),
                  pl.BlockSpec((1, ne), lambda i: (0, 0))],
        out_specs=pl.BlockSpec((8, ne), lambda i: (0, 0)),
        compiler_params=_cparams(("arbitrary",)),
        name="route_count",
    )(logits, rbias)


def _route_assign_kernel(lg_ref, rbias_ref, tri_ref, start_ref, dest_ref, gate_ref, seen_ref):
    tr = lg_ref.shape[0]

    @pl.when(pl.program_id(0) == 0)
    def _():
        seen_ref[...] = jnp.zeros_like(seen_ref)

    onehots, gates, chosen = _topk(lg_ref[...], rbias_ref[...])
    gsum = gates[0]
    for g in gates[1:]:
        gsum = gsum + g
    before = jnp.dot(tri_ref[...], chosen.astype(BF16), preferred_element_type=F32)
    slot = before + seen_ref[0:1, :] + start_ref[...]
    seen_ref[0:1, :] = seen_ref[0:1, :] + jnp.sum(chosen, axis=0, keepdims=True)

    out_lane = lax.broadcasted_iota(jnp.int32, (tr, OUT_LANES), 1)
    dest_o = jnp.zeros((tr, OUT_LANES), jnp.int32)
    gate_o = jnp.zeros((tr, OUT_LANES), F32)
    for k in range(TOP_K):
        dest = jnp.sum(jnp.where(onehots[k], slot, 0.0), axis=-1, keepdims=True)
        dest_o = jnp.where(out_lane == k, dest.astype(jnp.int32), dest_o)
        gate_o = jnp.where(out_lane == k, gates[k] / gsum * ROUTED_SCALING, gate_o)
    dest_ref[...] = dest_o
    gate_ref[...] = gate_o


def _route_assign(logits, rbias, tri, start):
    n, ne = logits.shape
    tr = tri.shape[0]
    return pl.pallas_call(
        _route_assign_kernel,
        out_shape=(jax.ShapeDtypeStruct((n, OUT_LANES), jnp.int32),
                   jax.ShapeDtypeStruct((n, OUT_LANES), F32)),
        grid=(n // tr,),
        in_specs=[pl.BlockSpec((tr, ne), lambda i: (i, 0)),
                  pl.BlockSpec((1, ne), lambda i: (0, 0)),
                  pl.BlockSpec((tr, tr), lambda i: (0, 0)),
                  pl.BlockSpec((1, ne), lambda i: (0, 0))],
        out_specs=(pl.BlockSpec((tr, OUT_LANES), lambda i: (i, 0)),
                   pl.BlockSpec((tr, OUT_LANES), lambda i: (i, 0))),
        scratch_shapes=[pltpu.VMEM((8, ne), F32)],
        compiler_params=_cparams(("arbitrary",)),
        name="route_assign",
    )(logits, rbias, tri, start)


def _dispatch_kernel(dest_ref, lo_ref, hi_ref, h_ref, xs_ref, zero_ref, sem, zsem):
    td = h_ref.shape[0]
    step = pl.program_id(0)
    n_exp = lo_ref.shape[0]
    tile = zero_ref.shape[0]

    @pl.when(step == 0)
    def _():
        zero_ref[...] = jnp.zeros_like(zero_ref)

        def zcopy(p):
            return pltpu.make_async_copy(zero_ref.at[pl.ds(0, 1)], xs_ref.at[pl.ds(p, 1)], zsem.at[0])

        def fill(e, c):
            lax.fori_loop(lo_ref[e], hi_ref[e], lambda p, c2: (zcopy(p).start(), c2)[1], 0)
            lax.fori_loop(lo_ref[e], hi_ref[e], lambda p, c2: (zcopy(p).wait(), c2)[1], 0)
            return c

        lax.fori_loop(0, n_exp, fill, 0)

        def tcopy(t):
            rows = pl.ds(pl.multiple_of(t * tile, tile), tile)
            return pltpu.make_async_copy(zero_ref, xs_ref.at[rows], zsem.at[1])

        first, last = hi_ref[n_exp - 1] // tile, xs_ref.shape[0] // tile
        lax.fori_loop(first, last, lambda t, c2: (tcopy(t).start(), c2)[1], 0)
        lax.fori_loop(first, last, lambda t, c2: (tcopy(t).wait(), c2)[1], 0)

    base = step * (td * TOP_K)

    def copy(t, k):
        return pltpu.make_async_copy(h_ref.at[pl.ds(t, 1)],
                                     xs_ref.at[pl.ds(dest_ref[base + t * TOP_K + k], 1)], sem.at[0])

    def start(t, c):
        for k in range(TOP_K):
            copy(t, k).start(priority=k % 2)
        return c

    def wait(t, c):
        for k in range(TOP_K):
            copy(t, k).wait()
        return c

    lax.fori_loop(0, td, start, 0)
    lax.fori_loop(0, td, wait, 0)


def _dispatch(dest_flat, fill_lo, fill_hi, h2, n_rows):
    n, d = h2.shape
    td = min(DISPATCH_TILE, n)
    return pl.pallas_call(
        _dispatch_kernel,
        out_shape=jax.ShapeDtypeStruct((n_rows, d), h2.dtype),
        grid_spec=pltpu.PrefetchScalarGridSpec(
            num_scalar_prefetch=3,
            grid=(n // td,),
            in_specs=[pl.BlockSpec((td, d), lambda i, *_: (i, 0))],
            out_specs=pl.BlockSpec(memory_space=pl.ANY),
            scratch_shapes=[pltpu.VMEM((MOE_TILE, d), h2.dtype),
                            pltpu.SemaphoreType.DMA((1,)),
                            pltpu.SemaphoreType.DMA((2,))]),
        compiler_params=_cparams(("arbitrary",)),
        name="dispatch",
    )(dest_flat, fill_lo, fill_hi, h2)


def _moe_kernel(te_ref, nu_ref, slot_ref, nxt_ref, ulo_ref, uhi_ref,
                x_ref, wg_hbm, wu_hbm, wd_hbm, y_ref,
                wg_bf, wu_bf, wd_bf, stage_gu, stage_d, sem):
    i = pl.program_id(0)
    d, f = wg_bf.shape[1], wg_bf.shape[2]
    ring, rg, rd = stage_gu.shape[0], stage_gu.shape[1], stage_d.shape[1]
    ahead = ring - 1
    ng, nd = d // rg, f // rd
    n_units = 2 * ng + nd
    kinds = ((0, ng, wg_hbm, wg_bf, stage_gu, rg), (ng, 2 * ng, wu_hbm, wu_bf, stage_gu, rg),
             (2 * ng, n_units, wd_hbm, wd_bf, stage_d, rd))

    def chunk_copy(kind, u, e):
        first, _, src, _, stage, rows = kind
        r0 = pl.multiple_of((u - first) * rows, rows)
        return pltpu.make_async_copy(src.at[e, pl.ds(r0, rows), :], stage.at[u % ring],
                                     sem.at[u % ring])

    def request(u, e):
        for kind in kinds:
            @pl.when((u >= kind[0]) & (u < kind[1]))
            def _(kind=kind):
                chunk_copy(kind, u, e).start()

    def receive(u, e, slot):
        for kind in kinds:
            @pl.when((u >= kind[0]) & (u < kind[1]))
            def _(kind=kind):
                first, _, _, dst, stage, rows = kind
                chunk_copy(kind, u, e).wait()
                r0 = pl.multiple_of((u - first) * rows, rows)
                dst[slot, pl.ds(r0, rows), :] = stage[u % ring].astype(BF16)

    def run_units(lo, hi, e, slot):
        def body(u, c):
            @pl.when(u == 0)
            def _():
                for j in range(ahead):
                    chunk_copy(kinds[0], j, e).start()

            @pl.when(u + ahead < n_units)
            def _():
                request(u + ahead, e)

            receive(u, e, slot)
            return c

        lax.fori_loop(lo, hi, body, 0)

    slot = slot_ref[i]

    @pl.when(i == 0)
    def _():
        run_units(0, n_units, te_ref[0], slot)

    run_units(ulo_ref[i], uhi_ref[i], nxt_ref[i], 1 - slot)

    @pl.when(i < nu_ref[0])
    def _():
        x = _unpack_halves(x_ref[...])
        hg = jnp.dot(x, wg_bf[slot], preferred_element_type=F32)
        hu = jnp.dot(x, wu_bf[slot], preferred_element_type=F32)
        h = (hg / (1.0 + jnp.exp(-hg))) * hu
        y_ref[...] = jnp.dot(h.astype(BF16), wd_bf[slot], preferred_element_type=F32)

    @pl.when(i >= nu_ref[0])
    def _():
        y_ref[...] = jnp.zeros_like(y_ref)


def _moe(tile_expert, n_used, tile_slot, tile_next, unit_lo, unit_hi, xs, wg, wu, wd):
    rows, dp = xs.shape
    _, d, f = wg.shape
    tm = MOE_TILE
    rg = rd = MOE_CHUNK_ROWS
    ring = min(MOE_RING, d // rg + 1)
    return pl.pallas_call(
        _moe_kernel,
        out_shape=jax.ShapeDtypeStruct((rows, d), F32),
        grid_spec=pltpu.PrefetchScalarGridSpec(
            num_scalar_prefetch=6,
            grid=(rows // tm,),
            in_specs=[pl.BlockSpec((tm, dp), lambda i, te, nu, *_: (jnp.minimum(i, nu[0] - 1), 0)),
                      pl.BlockSpec(memory_space=pl.ANY),
                      pl.BlockSpec(memory_space=pl.ANY),
                      pl.BlockSpec(memory_space=pl.ANY)],
            out_specs=pl.BlockSpec((tm, d), lambda i, *_: (i, 0)),
            scratch_shapes=[pltpu.VMEM((2, d, f), BF16),
                            pltpu.VMEM((2, d, f), BF16),
                            pltpu.VMEM((2, f, d), BF16),
                            pltpu.VMEM((ring, rg, f), F32),
                            pltpu.VMEM((ring, rd, d), F32),
                            pltpu.SemaphoreType.DMA((ring,))]),
        compiler_params=_cparams(("arbitrary",), V7X_VMEM_LIMIT_MOE),
        name="moe",
    )(tile_expert, n_used, tile_slot, tile_next, unit_lo, unit_hi, xs, wg, wu, wd)


def _shared_kernel(x_ref, wg_ref, wu_ref, wd_ref, y_ref):
    x = x_ref[...]
    hg = jnp.dot(x, wg_ref[...], preferred_element_type=F32)
    hu = jnp.dot(x, wu_ref[...], preferred_element_type=F32)
    h = (hg / (1.0 + jnp.exp(-hg))) * hu
    y_ref[...] = jnp.dot(h.astype(BF16), wd_ref[...], preferred_element_type=F32)


def _shared(h2b, wg, wu, wd):
    n, d = h2b.shape
    f = wg.shape[1]
    tm = min(512, n)
    const = lambda i: (0, 0)
    return pl.pallas_call(
        _shared_kernel,
        out_shape=jax.ShapeDtypeStruct((n, d), F32),
        grid=(n // tm,),
        in_specs=[pl.BlockSpec((tm, d), lambda i: (i, 0)),
                  pl.BlockSpec((d, f), const),
                  pl.BlockSpec((d, f), const),
                  pl.BlockSpec((f, d), const)],
        out_specs=pl.BlockSpec((tm, d), lambda i: (i, 0)),
        compiler_params=_cparams(("parallel",), V7X_VMEM_LIMIT),
        name="shared",
    )(h2b, wg, wu, wd)


def _combine_kernel(dest_ref, y_ref, gate_ref, sh_ref, x1_ref, mod_ref, g_ref, o_ref, buf_ref, sem):
    tc = x1_ref.shape[0]
    step = pl.program_id(0)
    n_steps = pl.num_programs(0)

    def copy(tile, slot, t, k):
        src = dest_ref[(tile * tc + t) * TOP_K + k]
        return pltpu.make_async_copy(y_ref.at[pl.ds(src, 1)], buf_ref.at[slot, k, pl.ds(t, 1)],
                                     sem.at[slot])

    def request(tile, slot):
        def body(t, c):
            for k in range(TOP_K):
                copy(tile, slot, t, k).start(priority=k % 2)
            return c
        lax.fori_loop(0, tc, body, 0)

    slot = step % 2

    @pl.when(step == 0)
    def _():
        request(step, slot)

    @pl.when(step + 1 < n_steps)
    def _():
        request(step + 1, 1 - slot)

    def await_(t, c):
        for k in range(TOP_K):
            copy(step, slot, t, k).wait()
        return c

    lax.fori_loop(0, tc, await_, 0)

    gate = gate_ref[...]
    y = sh_ref[...]
    for k in range(TOP_K):
        y = y + gate[:, k:k + 1] * buf_ref[slot, k]
    o_ref[...] = x1_ref[...] + mod_ref[5:6, :] * _rms(y, g_ref[...])


def _combine(dest_flat, y, gates, shared, x1, mod, g, seq):
    n, d = x1.shape
    tc = min(COMBINE_TILE, seq)
    per_b = seq // tc
    return pl.pallas_call(
        _combine_kernel,
        out_shape=jax.ShapeDtypeStruct((n, d), F32),
        grid_spec=pltpu.PrefetchScalarGridSpec(
            num_scalar_prefetch=1,
            grid=(n // tc,),
            in_specs=[pl.BlockSpec(memory_space=pl.ANY),
                      pl.BlockSpec((tc, OUT_LANES), lambda i, dest: (i, 0)),
                      pl.BlockSpec((tc, d), lambda i, dest: (i, 0)),
                      pl.BlockSpec((tc, d), lambda i, dest: (i, 0)),
                      pl.BlockSpec((None, 6, d), lambda i, dest: (i // per_b, 0, 0)),
                      pl.BlockSpec((1, d), lambda i, dest: (0, 0))],
            out_specs=pl.BlockSpec((tc, d), lambda i, dest: (i, 0)),
            scratch_shapes=[pltpu.VMEM((2, TOP_K, tc, d), F32),
                            pltpu.SemaphoreType.DMA((2,))]),
        compiler_params=_cparams(("arbitrary",), V7X_VMEM_LIMIT),
        name="combine",
    )(dest_flat, y, gates, shared, x1, mod, g)


def _t5_bucket(dist):
    max_exact = REL_BUCKETS // 2
    d_f = jnp.maximum(dist, max_exact).astype(F32)
    large = max_exact + (jnp.log(d_f / max_exact) / math.log(REL_MAX_DISTANCE / max_exact)
                         * (REL_BUCKETS - max_exact)).astype(jnp.int32)
    large = jnp.minimum(large, REL_BUCKETS - 1)
    return jnp.where(dist < max_exact, dist, large)


def _bucket_tables():
    i = jnp.arange(DSWA_BLK)[:, None]
    j = jnp.arange(2 * DSWA_BLK)[None, :]
    diff = jnp.maximum(i + DSWA_BLK - j, 0)
    return jnp.stack([_t5_bucket(diff * dil) for _, dil in DSWA_PATTERNS]).astype(jnp.int32)


def _strict_lower(n):
    return jnp.arange(n)[:, None] > jnp.arange(n)[None, :]


def _layer(x, c_pad, rel_bias, norm_pre_attn, norm_post_attn, norm_pre_ffn, norm_post_ffn,
           w_ada, b_ada, w_in, head_gain_dswa, head_gain_sb, w_out, w_router, router_bias,
           w_gate_e, w_up_e, w_down_e, w_gate_s, w_up_s, w_down_s):
    bsz, seq, d = x.shape
    n = bsz * seq
    n_heads = d // (2 * HEAD_DIM)
    n_exp = w_router.shape[1]
    row = lambda v: v.reshape(1, -1)

    mod = _ada(c_pad, w_ada, row(b_ada))[:bsz].reshape(bsz, 6, d)
    x2 = x.reshape(n, d)

    w_in_bf = w_in.astype(BF16)
    qkvq = _qkv(x2, mod, row(norm_pre_attn), w_in_bf, seq, 0, 2 * d, F32).reshape(bsz, seq, 2 * d)
    kv_sb = _qkv(x2, mod, row(norm_pre_attn), w_in_bf, seq, 2 * d, 3 * d, BF16).reshape(bsz, seq, d)

    oa = _dswa(rel_bias.T, qkvq, _bucket_tables(), row(head_gain_dswa), n_heads)
    tq = min(SB_TILE, seq)
    neg_tri = jnp.where(_strict_lower(tq), -1.0, 0.0).astype(BF16)
    ob = _sb(qkvq, kv_sb, jnp.concatenate([neg_tri, neg_tri], axis=0), row(head_gain_sb), n_heads,
             3 * n_heads)

    wr_hi = w_router.astype(BF16)
    wr_lo = (w_router - wr_hi.astype(F32)).astype(BF16)
    x1, h2p, h2b, logits = _outproj(oa.reshape(n, -1), ob.reshape(n, -1), w_out.astype(BF16), x2, mod,
                                   row(norm_post_attn), row(norm_pre_ffn), wr_hi, wr_lo, seq)

    tr = min(ROUTE_TILE, n)
    counts = _route_count(logits, row(router_bias), tr)[0].astype(jnp.int32)
    padded = (counts + MOE_TILE - 1) // MOE_TILE * MOE_TILE
    pad_end = jnp.cumsum(padded)
    pad_start = pad_end - padded
    n_rows = (n * TOP_K + n_exp * (MOE_TILE - 1) + MOE_TILE - 1) // MOE_TILE * MOE_TILE
    n_tiles = n_rows // MOE_TILE
    n_used = jnp.maximum(pad_end[-1] // MOE_TILE, 1).astype(jnp.int32)
    tile_id = jnp.minimum(jnp.arange(n_tiles, dtype=jnp.int32), n_used - 1)
    tile_expert = jnp.minimum(
        jnp.sum(pad_end[None, :] <= (tile_id * MOE_TILE)[:, None], axis=1), n_exp - 1).astype(jnp.int32)

    n_units = (2 * d + w_gate_e.shape[2]) // MOE_CHUNK_ROWS
    nonempty = padded > 0
    ids = jnp.arange(n_exp, dtype=jnp.int32)
    later = lax.cummin(jnp.where(nonempty, ids, n_exp)[::-1])[::-1]
    next_e = jnp.concatenate([later[1:], jnp.full((1,), n_exp, jnp.int32)])
    order = jnp.cumsum(nonempty.astype(jnp.int32)) - 1
    mine = tile_expert[:, None] == ids[None, :]
    per_tile = lambda v: jnp.sum(jnp.where(mine, v[None, :], 0), axis=1)
    t_in = tile_id - per_tile(pad_start // MOE_TILE)
    t_cnt = jnp.maximum(per_tile(padded // MOE_TILE), 1)
    nxt = per_tile(next_e)
    live = (jnp.arange(n_tiles) < n_used) & (nxt < n_exp)
    unit_lo = jnp.where(live, n_units * t_in // t_cnt, 0).astype(jnp.int32)
    unit_hi = jnp.where(live, n_units * (t_in + 1) // t_cnt, 0).astype(jnp.int32)
    tile_next = jnp.minimum(nxt, n_exp - 1).astype(jnp.int32)
    tile_slot = (per_tile(order) % 2).astype(jnp.int32)

    dest, gates = _route_assign(logits, row(router_bias), _strict_lower(tr).astype(BF16),
                                row(pad_start.astype(F32)))
    dest_flat = dest[:, :TOP_K].reshape(-1)

    xs = _dispatch(dest_flat, (pad_start + counts).astype(jnp.int32), pad_end.astype(jnp.int32), h2p, n_rows)
    y = _moe(tile_expert, n_used.reshape(1), tile_slot, tile_next, unit_lo, unit_hi, xs,
             w_gate_e, w_up_e, w_down_e)
    shared = _shared(h2b, w_gate_s.astype(BF16), w_up_s.astype(BF16), w_down_s.astype(BF16))
    out = _combine(dest_flat, y, gates, shared, x1, mod, row(norm_post_ffn), seq)
    return out.reshape(bsz, seq, d)


def kernel(x, c, rel_bias, norm_pre_attn, norm_post_attn, norm_pre_ffn, norm_post_ffn, w_ada, b_ada, w_in, head_gain_dswa, head_gain_sb, w_out, w_router, router_bias, w_gate_e, w_up_e, w_down_e, w_gate_s, w_up_s, w_down_s):
    bsz = x.shape[0]
    depth = w_ada.shape[0]
    c_pad = jnp.zeros((8, c.shape[1]), F32).at[:bsz].set(c)
    for l in range(depth):
        x = _layer(x, c_pad, rel_bias, norm_pre_attn[l], norm_post_attn[l], norm_pre_ffn[l],
                   norm_post_ffn[l], w_ada[l], b_ada[l], w_in[l], head_gain_dswa[l], head_gain_sb[l],
                   w_out[l], w_router[l], router_bias[l], w_gate_e[l], w_up_e[l], w_down_e[l],
                   w_gate_s[l], w_up_s[l], w_down_s[l])
    return x
```

```python
import math

import jax
import jax.numpy as jnp
from jax import lax
from jax.experimental import pallas as pl
from jax.experimental.pallas import tpu as pltpu

F32 = jnp.float32
BF16 = jnp.bfloat16

HEAD_DIM = 128
DSWA_PATTERNS = ((128, 1), (512, 4), (2048, 16))
DSWA_BLK = 128
DSWA_GROUP = 4
REL_BUCKETS = 32
REL_MAX_DISTANCE = 2048
TOP_K = 6
ROUTED_SCALING = 2.446
NORM_EPS = 1e-6
NEG_BIG = -1e30
LOG2E = 1.4426950408889634

V7X_VMEM_LIMIT = 56 * 1024 * 1024
V7X_VMEM_LIMIT_MOE = 60 * 1024 * 1024
MOE_TILE = 256
MOE_CHUNK_ROWS = 128
MOE_RING = 8
SB_TILE = 256
SB_HEADS = 4
SB_ROW_SPLIT = 2
SB_UNDERFLOW_LOG2 = -160.0
ROUTE_TILE = 512
DISPATCH_TILE = 256
COMBINE_TILE = 128
OUT_LANES = 128


def _cparams(sem, vmem=None):
    return pltpu.CompilerParams(dimension_semantics=sem, vmem_limit_bytes=vmem)


def _rms(x, g):
    return x * lax.rsqrt(jnp.mean(x * x, axis=-1, keepdims=True) + NORM_EPS) * g


def _ada_kernel(c_ref, w_ref, b_ref, o_ref):
    c = c_ref[...]
    ca = c / (1.0 + jnp.exp(-c))
    o_ref[...] = jnp.dot(ca.astype(BF16), w_ref[...].astype(BF16),
                         preferred_element_type=F32) + b_ref[...]


def _ada(c_pad, w_ada, b_ada):
    rows, d = c_pad.shape
    n = w_ada.shape[1]
    tn = min(n, 1536)
    return pl.pallas_call(
        _ada_kernel,
        out_shape=jax.ShapeDtypeStruct((rows, n), F32),
        grid=(n // tn,),
        in_specs=[pl.BlockSpec((rows, d), lambda j: (0, 0)),
                  pl.BlockSpec((d, tn), lambda j: (0, j)),
                  pl.BlockSpec((1, tn), lambda j: (0, j))],
        out_specs=pl.BlockSpec((rows, tn), lambda j: (0, j)),
        compiler_params=_cparams(("arbitrary",), V7X_VMEM_LIMIT),
        name="ada",
    )(c_pad, w_ada, b_ada)


def _qkv_kernel(x_ref, mod_ref, g_ref, w_ref, o_ref, h_ref):
    @pl.when(pl.program_id(1) == 0)
    def _():
        h = _rms(x_ref[...], g_ref[...]) * (1.0 + mod_ref[1:2, :]) + mod_ref[0:1, :]
        h_ref[...] = h.astype(BF16)

    o_ref[...] = jnp.dot(h_ref[...], w_ref[...], preferred_element_type=F32).astype(o_ref.dtype)


def _qkv(x2, mod, g, w_bf, seq, col_lo, col_hi, out_dtype):
    n, d = x2.shape
    nout = col_hi - col_lo
    tm = min(1024, seq)
    tn = min(1024, nout)
    j0 = col_lo // tn
    per_b = seq // tm
    return pl.pallas_call(
        _qkv_kernel,
        out_shape=jax.ShapeDtypeStruct((n, nout), out_dtype),
        grid=(n // tm, nout // tn),
        in_specs=[pl.BlockSpec((tm, d), lambda i, j: (i, 0)),
                  pl.BlockSpec((None, 6, d), lambda i, j: (i // per_b, 0, 0)),
                  pl.BlockSpec((1, d), lambda i, j: (0, 0)),
                  pl.BlockSpec((d, tn), lambda i, j: (0, j0 + j))],
        out_specs=pl.BlockSpec((tm, tn), lambda i, j: (i, j)),
        scratch_shapes=[pltpu.VMEM((tm, d), BF16)],
        compiler_params=_cparams(("parallel", "arbitrary"), V7X_VMEM_LIMIT),
        name="qkv",
    )(x2, mod, g, w_bf)


def _dswa_kernel(rb_ref, q_ref, k_ref, v_ref, bucket_ref, g_ref, o_ref, *stats):
    n_pat = len(DSWA_PATTERNS)
    acc_refs, m_refs, l_refs = stats[:n_pat], stats[n_pat:2 * n_pat], stats[2 * n_pat:]
    seq = q_ref.shape[0]
    head = pl.program_id(1)
    scale = 1.0 / math.sqrt(HEAD_DIM)
    blk = DSWA_BLK
    ii = lax.broadcasted_iota(jnp.int32, (blk, 2 * blk), 0)
    jj = lax.broadcasted_iota(jnp.int32, (blk, 2 * blk), 1)
    diff = ii + blk - jj
    band = (diff >= 0) & (diff <= blk)

    for p, (_, dil) in enumerate(DSWA_PATTERNS):
        nb = seq // dil // blk
        bucket = bucket_ref[p]
        bias = jnp.zeros((blk, 2 * blk), F32)
        for b in range(REL_BUCKETS):
            bias = jnp.where(bucket == b, rb_ref[head, b], bias)
        bias = jnp.where(band, bias, NEG_BIG)
        acc_ref, m_ref, l_ref = acc_refs[p], m_refs[p], l_refs[p]

        def rows(start, size, dil=dil):
            if dil == 1:
                return pl.ds(start, size)
            return pl.ds(start, size, stride=dil)

        def group(blocks, bias=bias, rows=rows, dil=dil,
                  acc_ref=acc_ref, m_ref=m_ref, l_ref=l_ref):
            rqs, rks, bss = [], [], []
            for r, n in blocks:
                rq = rows(n * blk * dil + r, blk)
                rqs.append(rq)
                rks.append(rq if n == 0 else rows((n - 1) * blk * dil + r, 2 * blk))
                bss.append(bias[:, blk:] if n == 0 else bias)
            qs = [(q_ref[rq, :] * scale).astype(BF16) for rq in rqs]
            ks = [k_ref[rk, :].astype(BF16) for rk in rks]
            ss = [lax.dot_general(q, k, (((1,), (1,)), ((), ())), preferred_element_type=F32) + bs
                  for q, k, bs in zip(qs, ks, bss)]
            ms = [jnp.max(s, axis=-1, keepdims=True) for s in ss]
            ps = [jnp.exp(s - m) for s, m in zip(ss, ms)]
            vs = [v_ref[rk, :].astype(BF16) for rk in rks]
            for rq, m, pexp, v in zip(rqs, ms, ps, vs):
                acc_ref[rq, :] = jnp.dot(pexp.astype(BF16), v, preferred_element_type=F32)
                m_ref[rq, :] = jnp.broadcast_to(m, (blk, HEAD_DIM))
                l_ref[rq, :] = jnp.broadcast_to(jnp.sum(pexp, axis=-1, keepdims=True),
                                                (blk, HEAD_DIM))

        per_phase = min(nb, DSWA_GROUP)
        phases = max(1, min(dil, DSWA_GROUP // per_phase))

        def trip(t, carry, nb=nb, per_phase=per_phase, phases=phases, group=group):
            for n0 in range(0, nb, per_phase):
                group([(t * phases + dr, n0 + dn) for dr in range(phases) for dn in range(per_phase)])
            return carry

        lax.fori_loop(0, dil // phases, trip, 0)

    chunk = min(256, seq)

    def finish(c, carry):
        rs = pl.ds(pl.multiple_of(c * chunk, chunk), chunk)
        ms = [m_ref[rs, :] for m_ref in m_refs]
        m_all = ms[0]
        for m in ms[1:]:
            m_all = jnp.maximum(m_all, m)
        num = jnp.zeros((chunk, HEAD_DIM), F32)
        den = jnp.zeros((chunk, HEAD_DIM), F32)
        for p in range(n_pat):
            w = jnp.exp(ms[p] - m_all)
            num = num + w * acc_refs[p][rs, :]
            den = den + w * l_refs[p][rs, :]
        o_ref[rs, :] = _rms(num / den, g_ref[...]).astype(o_ref.dtype)
        return carry

    lax.fori_loop(0, seq // chunk, finish, 0)


def _dswa(rel_bias_t, qkv3, buckets, gain, n_heads):
    bsz, seq, _ = qkv3.shape
    w = n_heads * HEAD_DIM
    return pl.pallas_call(
        _dswa_kernel,
        out_shape=jax.ShapeDtypeStruct((bsz, seq, w), BF16),
        grid=(bsz, n_heads),
        in_specs=[pl.BlockSpec(memory_space=pltpu.SMEM),
                  pl.BlockSpec((None, seq, HEAD_DIM), lambda b, h: (b, 0, h)),
                  pl.BlockSpec((None, seq, HEAD_DIM), lambda b, h: (b, 0, n_heads + h)),
                  pl.BlockSpec((None, seq, HEAD_DIM), lambda b, h: (b, 0, 2 * n_heads + h)),
                  pl.BlockSpec(buckets.shape, lambda b, h: (0, 0, 0)),
                  pl.BlockSpec((1, HEAD_DIM), lambda b, h: (0, h))],
        out_specs=pl.BlockSpec((None, seq, HEAD_DIM), lambda b, h: (b, 0, h)),
        scratch_shapes=[pltpu.VMEM((seq, HEAD_DIM), F32)] * (3 * len(DSWA_PATTERNS)),
        compiler_params=_cparams(("parallel", "parallel"), V7X_VMEM_LIMIT),
        name="dswa",
    )(rel_bias_t, qkv3, qkv3, qkv3, buckets, gain)


def _neg_abs(z):
    bits = lax.bitcast_convert_type(z, jnp.uint32) | jnp.uint32(0x80000000)
    return lax.bitcast_convert_type(bits, F32)


def _sb_kernel(q_ref, kb_ref, vb_ref, tri_ref, g_ref, o_ref):
    tq = q_ref.shape[0]
    nh = q_ref.shape[1] // HEAD_DIM
    iq = pl.program_id(2)

    q_all = (q_ref[...] * (LOG2E / math.sqrt(HEAD_DIM))).astype(BF16)
    rs = tq // SB_ROW_SPLIT
    ti = lax.broadcasted_iota(jnp.int32, (tq, tq), 0)
    si = lax.broadcasted_iota(jnp.int32, (tq, tq), 1)
    causal = si < ti
    chains = [(h, s) for h in range(nh) for s in range(SB_ROW_SPLIT)]
    cols = [slice(h * HEAD_DIM, (h + 1) * HEAD_DIM) for h, _ in chains]
    rws = [slice(s * rs, (s + 1) * rs) for _, s in chains]
    qs = [q_all[r, c] for r, c in zip(rws, cols)]
    masks = [causal[r, :] for r in rws]

    def visit(jk, state, masked):
        r0 = pl.multiple_of(jk * tq, tq)
        carries, accs = state
        ks = [kb_ref[pl.ds(r0, tq), c] for c in cols]
        vs = [vb_ref[pl.ds(r0, tq), c] for c in cols]
        zs = [lax.dot_general(q, k, (((1,), (1,)), ((), ())), preferred_element_type=F32)
              for q, k in zip(qs, ks)]
        his, los, logsig, rowsums = [], [], [], []
        for z, m in zip(zs, masks):
            sp = jnp.maximum(z, 0.0) + jnp.log(1.0 + jnp.exp2(_neg_abs(z))) * LOG2E
            if masked:
                sp = jnp.where(m, sp, 0.0)
            hi = sp.astype(BF16)
            his.append(hi)
            los.append((sp - hi.astype(F32)).astype(BF16))
            logsig.append(z - sp)
            rowsums.append(jnp.sum(sp, axis=-1, keepdims=True))
        within = [jnp.dot(jnp.concatenate([hi, lo], axis=1), tri_ref[...], preferred_element_type=F32)
                  for hi, lo in zip(his, los)]
        probs = [jnp.exp2(ls + wi + carry) for ls, wi, carry in zip(logsig, within, carries)]
        if masked:
            probs = [jnp.where(m, a, 0.0) for m, a in zip(masks, probs)]
        accs = tuple(acc + jnp.dot(a.astype(BF16), v, preferred_element_type=F32)
                     for acc, a, v in zip(accs, probs, vs))
        carries = tuple(carry - rsum for carry, rsum in zip(carries, rowsums))
        return carries, accs

    def remaining(carries):
        left = carries[0]
        for c in carries[1:]:
            left = jnp.maximum(left, c)
        return jnp.max(left)

    state = (tuple(jnp.zeros((rs, 1), F32) for _ in chains),
             tuple(jnp.zeros((rs, HEAD_DIM), F32) for _ in chains))
    state = visit(iq, state, True)

    def more(loop):
        t, left, _ = loop
        return (t < iq) & (left > SB_UNDERFLOW_LOG2)

    def step(loop):
        t, _, st = loop
        st = visit(iq - 1 - t, st, False)
        return t + 1, remaining(st[0]), st

    _, _, state = lax.while_loop(more, step, (jnp.int32(0), remaining(state[0]), state))
    for r, c, acc in zip(rws, cols, state[1]):
        o_ref[r, c] = _rms(acc, g_ref[:, c]).astype(o_ref.dtype)


def _sb(q_src, kv, tri2, gain, n_heads, q_col0):
    bsz, seq, _ = kv.shape
    tq = min(SB_TILE, seq)
    nh = min(SB_HEADS, n_heads)
    wblk = nh * HEAD_DIM
    groups = n_heads // nh
    c0 = q_col0 // nh
    return pl.pallas_call(
        _sb_kernel,
        out_shape=jax.ShapeDtypeStruct((bsz, seq, n_heads * HEAD_DIM), BF16),
        grid=(bsz, groups, seq // tq),
        in_specs=[pl.BlockSpec((None, tq, wblk), lambda b, h, i: (b, i, c0 + h)),
                  pl.BlockSpec((None, seq, wblk), lambda b, h, i: (b, 0, h)),
                  pl.BlockSpec((None, seq, wblk), lambda b, h, i: (b, 0, groups + h)),
                  pl.BlockSpec(tri2.shape, lambda b, h, i: (0, 0)),
                  pl.BlockSpec((1, wblk), lambda b, h, i: (0, h))],
        out_specs=pl.BlockSpec((None, tq, wblk), lambda b, h, i: (b, i, h)),
        compiler_params=_cparams(("parallel", "parallel", "arbitrary"), V7X_VMEM_LIMIT),
        name="sb",
    )(q_src, kv, kv, tri2, gain)


def _pack_halves(v):
    c = v.shape[1] // 2
    bits = lax.bitcast_convert_type(v, jnp.uint32)
    return (bits[:, :c] >> 16) | (bits[:, c:] & jnp.uint32(0xFFFF0000))


def _unpack_halves(p):
    lo = lax.bitcast_convert_type(p << 16, F32).astype(BF16)
    hi = lax.bitcast_convert_type(p & jnp.uint32(0xFFFF0000), F32).astype(BF16)
    return jnp.concatenate([lo, hi], axis=1)


def _outproj_kernel(oa_ref, ob_ref, w_ref, x_ref, mod_ref, gpa_ref, gpf_ref, wr_hi_ref, wr_lo_ref,
                    x1_ref, h2p_ref, lg_ref):
    wa = oa_ref.shape[1]
    o = jnp.dot(oa_ref[...], w_ref[0:wa, :], preferred_element_type=F32)
    o = o + jnp.dot(ob_ref[...], w_ref[wa:, :], preferred_element_type=F32)
    x1 = x_ref[...] + mod_ref[2:3, :] * _rms(o, gpa_ref[...])
    x1_ref[...] = x1
    h2 = _rms(x1, gpf_ref[...]) * (1.0 + mod_ref[4:5, :]) + mod_ref[3:4, :]
    hi = h2.astype(BF16)
    hi_f = hi.astype(F32)
    h2p_ref[...] = _pack_halves(hi_f)
    lo = (h2 - hi_f).astype(BF16)
    lg = jnp.dot(hi, wr_hi_ref[...], preferred_element_type=F32)
    lg = lg + jnp.dot(lo, wr_hi_ref[...], preferred_element_type=F32)
    lg = lg + jnp.dot(hi, wr_lo_ref[...], preferred_element_type=F32)
    lg_ref[...] = lg


def _outproj(oa, ob, w_bf, x2, mod, gpa, gpf, wr_hi, wr_lo, seq):
    n, d = x2.shape
    wa = oa.shape[1]
    ne = wr_hi.shape[1]
    tm = min(256, seq)
    per_b = seq // tm
    const = lambda i: (0, 0)
    return pl.pallas_call(
        _outproj_kernel,
        out_shape=(jax.ShapeDtypeStruct((n, d), F32), jax.ShapeDtypeStruct((n, d // 2), jnp.uint32),
                   jax.ShapeDtypeStruct((n, ne), F32)),
        grid=(n // tm,),
        in_specs=[pl.BlockSpec((tm, wa), lambda i: (i, 0)),
                  pl.BlockSpec((tm, wa), lambda i: (i, 0)),
                  pl.BlockSpec(w_bf.shape, const),
                  pl.BlockSpec((tm, d), lambda i: (i, 0)),
                  pl.BlockSpec((None, 6, d), lambda i: (i // per_b, 0, 0)),
                  pl.BlockSpec((1, d), const),
                  pl.BlockSpec((1, d), const),
                  pl.BlockSpec(wr_hi.shape, const),
                  pl.BlockSpec(wr_lo.shape, const)],
        out_specs=(pl.BlockSpec((tm, d), lambda i: (i, 0)),
                   pl.BlockSpec((tm, d // 2), lambda i: (i, 0)),
                   pl.BlockSpec((tm, ne), lambda i: (i, 0))),
        compiler_params=_cparams(("parallel",), V7X_VMEM_LIMIT),
        name="outproj",
    )(oa, ob, w_bf, x2, mod, gpa, gpf, wr_hi, wr_lo)


def _topk(lg, rbias):
    tr, ne = lg.shape
    scores = 1.0 / (1.0 + jnp.exp(-lg))
    sel = scores + rbias
    lane = lax.broadcasted_iota(jnp.int32, (tr, ne), 1).astype(F32)
    onehots, gates = [], []
    chosen = jnp.zeros((tr, ne), F32)
    for _ in range(TOP_K):
        m = jnp.max(sel, axis=-1, keepdims=True)
        idx = jnp.min(jnp.where(sel == m, lane, float(ne)), axis=-1, keepdims=True)
        oh = lane == idx
        gates.append(jnp.sum(jnp.where(oh, scores, 0.0), axis=-1, keepdims=True))
        sel = jnp.where(oh, -jnp.inf, sel)
        chosen = jnp.where(oh, 1.0, chosen)
        onehots.append(oh)
    return onehots, gates, chosen


def _route_count_kernel(lg_ref, rbias_ref, cnt_ref):
    @pl.when(pl.program_id(0) == 0)
    def _():
        cnt_ref[...] = jnp.zeros_like(cnt_ref)

    _, _, chosen = _topk(lg_ref[...], rbias_ref[...])
    cnt_ref[0:1, :] = cnt_ref[0:1, :] + jnp.sum(chosen, axis=0, keepdims=True)


def _route_count(logits, rbias, tr):
    n, ne = logits.shape
    return pl.pallas_call(
        _route_count_kernel,
        out_shape=jax.ShapeDtypeStruct((8, ne), F32),
        grid=(n // tr,),
        in_specs=[pl.BlockSpec((tr, ne), lambda i: (i, 0)),
                  pl.BlockSpec((1, ne), lambda i: (0, 0))],
        out_specs=pl.BlockSpec((8, ne), lambda i: (0, 0)),
        compiler_params=_cparams(("arbitrary",)),
        name="route_count",
    )(logits, rbias)


def _route_assign_kernel(lg_ref, rbias_ref, tri_ref, start_ref, dest_ref, gate_ref, seen_ref):
    tr = lg_ref.shape[0]

    @pl.when(pl.program_id(0) == 0)
    def _():
        seen_ref[...] = jnp.zeros_like(seen_ref)

    onehots, gates, chosen = _topk(lg_ref[...], rbias_ref[...])
    gsum = gates[0]
    for g in gates[1:]:
        gsum = gsum + g
    before = jnp.dot(tri_ref[...], chosen.astype(BF16), preferred_element_type=F32)
    slot = before + seen_ref[0:1, :] + start_ref[...]
    seen_ref[0:1, :] = seen_ref[0:1, :] + jnp.sum(chosen, axis=0, keepdims=True)

    out_lane = lax.broadcasted_iota(jnp.int32, (tr, OUT_LANES), 1)
    dest_o = jnp.zeros((tr, OUT_LANES), jnp.int32)
    gate_o = jnp.zeros((tr, OUT_LANES), F32)
    for k in range(TOP_K):
        dest = jnp.sum(jnp.where(onehots[k], slot, 0.0), axis=-1, keepdims=True)
        dest_o = jnp.where(out_lane == k, dest.astype(jnp.int32), dest_o)
        gate_o = jnp.where(out_lane == k, gates[k] / gsum * ROUTED_SCALING, gate_o)
    dest_ref[...] = dest_o
    gate_ref[...] = gate_o


def _route_assign(logits, rbias, tri, start):
    n, ne = logits.shape
    tr = tri.shape[0]
    return pl.pallas_call(
        _route_assign_kernel,
        out_shape=(jax.ShapeDtypeStruct((n, OUT_LANES), jnp.int32),
                   jax.ShapeDtypeStruct((n, OUT_LANES), F32)),
        grid=(n // tr,),
        in_specs=[pl.BlockSpec((tr, ne), lambda i: (i, 0)),
                  pl.BlockSpec((1, ne), lambda i: (0, 0)),
                  pl.BlockSpec((tr, tr), lambda i: (0, 0)),
                  pl.BlockSpec((1, ne), lambda i: (0, 0))],
        out_specs=(pl.BlockSpec((tr, OUT_LANES), lambda i: (i, 0)),
                   pl.BlockSpec((tr, OUT_LANES), lambda i: (i, 0))),
        scratch_shapes=[pltpu.VMEM((8, ne), F32)],
        compiler_params=_cparams(("arbitrary",)),
        name="route_assign",
    )(logits, rbias, tri, start)


def _dispatch_shared_kernel(dest_ref, lo_ref, hi_ref, h_ref, wg_ref, wu_ref, wd_ref,
                            y_ref, xs_ref, zero_ref, sem, zsem):
    td = h_ref.shape[0]
    step = pl.program_id(0)
    n_exp = lo_ref.shape[0]
    tile = zero_ref.shape[0]

    @pl.when(step == 0)
    def _():
        zero_ref[...] = jnp.zeros_like(zero_ref)

        def zcopy(p):
            return pltpu.make_async_copy(zero_ref.at[pl.ds(0, 1)], xs_ref.at[pl.ds(p, 1)], zsem.at[0])

        def fill(e, c):
            lax.fori_loop(lo_ref[e], hi_ref[e], lambda p, c2: (zcopy(p).start(), c2)[1], 0)
            lax.fori_loop(lo_ref[e], hi_ref[e], lambda p, c2: (zcopy(p).wait(), c2)[1], 0)
            return c

        lax.fori_loop(0, n_exp, fill, 0)

        def tcopy(t):
            rows = pl.ds(pl.multiple_of(t * tile, tile), tile)
            return pltpu.make_async_copy(zero_ref, xs_ref.at[rows], zsem.at[1])

        first, last = hi_ref[n_exp - 1] // tile, xs_ref.shape[0] // tile
        lax.fori_loop(first, last, lambda t, c2: (tcopy(t).start(), c2)[1], 0)
        lax.fori_loop(first, last, lambda t, c2: (tcopy(t).wait(), c2)[1], 0)

    base = step * (td * TOP_K)

    def copy(t, k):
        return pltpu.make_async_copy(h_ref.at[pl.ds(t, 1)],
                                     xs_ref.at[pl.ds(dest_ref[base + t * TOP_K + k], 1)], sem.at[0])

    for t in range(td):
        for k in range(TOP_K):
            copy(t, k).start()

    x = _unpack_halves(h_ref[...])
    hg = jnp.dot(x, wg_ref[...], preferred_element_type=F32)
    hu = jnp.dot(x, wu_ref[...], preferred_element_type=F32)
    h = (hg / (1.0 + jnp.exp(-hg))) * hu
    y_ref[...] = jnp.dot(h.astype(BF16), wd_ref[...], preferred_element_type=F32)

    for t in range(td):
        for k in range(TOP_K):
            copy(t, k).wait()


def _dispatch_shared(dest_flat, fill_lo, fill_hi, h2p, n_rows, wg, wu, wd):
    n, dp = h2p.shape
    d, f = wg.shape
    td = min(DISPATCH_TILE, n)
    const = lambda i, *_: (0, 0)
    return pl.pallas_call(
        _dispatch_shared_kernel,
        out_shape=(jax.ShapeDtypeStruct((n, d), F32), jax.ShapeDtypeStruct((n_rows, dp), h2p.dtype)),
        grid_spec=pltpu.PrefetchScalarGridSpec(
            num_scalar_prefetch=3,
            grid=(n // td,),
            in_specs=[pl.BlockSpec((td, dp), lambda i, *_: (i, 0)),
                      pl.BlockSpec((d, f), const),
                      pl.BlockSpec((d, f), const),
                      pl.BlockSpec((f, d), const)],
            out_specs=(pl.BlockSpec((td, d), lambda i, *_: (i, 0)),
                       pl.BlockSpec(memory_space=pl.ANY)),
            scratch_shapes=[pltpu.VMEM((MOE_TILE, dp), h2p.dtype),
                            pltpu.SemaphoreType.DMA((1,)),
                            pltpu.SemaphoreType.DMA((2,))]),
        compiler_params=_cparams(("arbitrary",), V7X_VMEM_LIMIT),
        name="dispatch_shared",
    )(dest_flat, fill_lo, fill_hi, h2p, wg, wu, wd)


def _moe_kernel(te_ref, nu_ref, slot_ref, nxt_ref, ulo_ref, uhi_ref,
                x_ref, wg_hbm, wu_hbm, wd_hbm, y_ref,
                wg_bf, wu_bf, wd_bf, stage_gu, stage_d, sem):
    i = pl.program_id(0)
    d, f = wg_bf.shape[1], wg_bf.shape[2]
    ring, rg, rd = stage_gu.shape[0], stage_gu.shape[1], stage_d.shape[1]
    ahead = ring - 1
    ng, nd = d // rg, f // rd
    n_units = 2 * ng + nd
    kinds = ((0, ng, wg_hbm, wg_bf, stage_gu, rg), (ng, 2 * ng, wu_hbm, wu_bf, stage_gu, rg),
             (2 * ng, n_units, wd_hbm, wd_bf, stage_d, rd))

    def chunk_copy(kind, u, e):
        first, _, src, _, stage, rows = kind
        r0 = pl.multiple_of((u - first) * rows, rows)
        return pltpu.make_async_copy(src.at[e, pl.ds(r0, rows), :], stage.at[u % ring],
                                     sem.at[u % ring])

    def request(u, e):
        for kind in kinds:
            @pl.when((u >= kind[0]) & (u < kind[1]))
            def _(kind=kind):
                chunk_copy(kind, u, e).start()

    def receive(u, e, slot):
        for kind in kinds:
            @pl.when((u >= kind[0]) & (u < kind[1]))
            def _(kind=kind):
                first, _, _, dst, stage, rows = kind
                chunk_copy(kind, u, e).wait()
                r0 = pl.multiple_of((u - first) * rows, rows)
                dst[slot, pl.ds(r0, rows), :] = stage[u % ring].astype(BF16)

    def run_units(lo, hi, e, slot):
        def body(u, c):
            @pl.when(u == 0)
            def _():
                for j in range(ahead):
                    chunk_copy(kinds[0], j, e).start()

            @pl.when(u + ahead < n_units)
            def _():
                request(u + ahead, e)

            receive(u, e, slot)
            return c

        lax.fori_loop(lo, hi, body, 0)

    slot = slot_ref[i]

    @pl.when(i == 0)
    def _():
        run_units(0, n_units, te_ref[0], slot)

    run_units(ulo_ref[i], uhi_ref[i], nxt_ref[i], 1 - slot)

    @pl.when(i < nu_ref[0])
    def _():
        x = _unpack_halves(x_ref[...])
        hg = jnp.dot(x, wg_bf[slot], preferred_element_type=F32)
        hu = jnp.dot(x, wu_bf[slot], preferred_element_type=F32)
        h = (hg / (1.0 + jnp.exp(-hg))) * hu
        y_ref[...] = jnp.dot(h.astype(BF16), wd_bf[slot], preferred_element_type=F32)

    @pl.when(i >= nu_ref[0])
    def _():
        y_ref[...] = jnp.zeros_like(y_ref)


def _moe(tile_expert, n_used, tile_slot, tile_next, unit_lo, unit_hi, xs, wg, wu, wd):
    rows, dp = xs.shape
    _, d, f = wg.shape
    tm = MOE_TILE
    rg = rd = MOE_CHUNK_ROWS
    ring = min(MOE_RING, d // rg + 1)
    return pl.pallas_call(
        _moe_kernel,
        out_shape=jax.ShapeDtypeStruct((rows, d), F32),
        grid_spec=pltpu.PrefetchScalarGridSpec(
            num_scalar_prefetch=6,
            grid=(rows // tm,),
            in_specs=[pl.BlockSpec((tm, dp), lambda i, te, nu, *_: (jnp.minimum(i, nu[0] - 1), 0)),
                      pl.BlockSpec(memory_space=pl.ANY),
                      pl.BlockSpec(memory_space=pl.ANY),
                      pl.BlockSpec(memory_space=pl.ANY)],
            out_specs=pl.BlockSpec((tm, d), lambda i, *_: (i, 0)),
            scratch_shapes=[pltpu.VMEM((2, d, f), BF16),
                            pltpu.VMEM((2, d, f), BF16),
                            pltpu.VMEM((2, f, d), BF16),
                            pltpu.VMEM((ring, rg, f), F32),
                            pltpu.VMEM((ring, rd, d), F32),
                            pltpu.SemaphoreType.DMA((ring,))]),
        compiler_params=_cparams(("arbitrary",), V7X_VMEM_LIMIT_MOE),
        name="moe",
    )(tile_expert, n_used, tile_slot, tile_next, unit_lo, unit_hi, xs, wg, wu, wd)


def _combine_kernel(dest_ref, y_ref, gate_ref, sh_ref, x1_ref, mod_ref, g_ref, o_ref, buf_ref, sem):
    tc = x1_ref.shape[0]
    step = pl.program_id(0)
    n_steps = pl.num_programs(0)

    def copy(tile, slot, t, k):
        src = dest_ref[(tile * tc + t) * TOP_K + k]
        return pltpu.make_async_copy(y_ref.at[pl.ds(src, 1)], buf_ref.at[slot, k, pl.ds(t, 1)],
                                     sem.at[slot])

    def request(tile, slot):
        def body(t, c):
            for k in range(TOP_K):
                copy(tile, slot, t, k).start()
            return c
        lax.fori_loop(0, tc, body, 0)

    slot = step % 2

    @pl.when(step == 0)
    def _():
        request(step, slot)

    @pl.when(step + 1 < n_steps)
    def _():
        request(step + 1, 1 - slot)

    def await_(t, c):
        for k in range(TOP_K):
            copy(step, slot, t, k).wait()
        return c

    lax.fori_loop(0, tc, await_, 0)

    gate = gate_ref[...]
    y = sh_ref[...]
    for k in range(TOP_K):
        y = y + gate[:, k:k + 1] * buf_ref[slot, k]
    o_ref[...] = x1_ref[...] + mod_ref[5:6, :] * _rms(y, g_ref[...])


def _combine(dest_flat, y, gates, shared, x1, mod, g, seq):
    n, d = x1.shape
    tc = min(COMBINE_TILE, seq)
    per_b = seq // tc
    return pl.pallas_call(
        _combine_kernel,
        out_shape=jax.ShapeDtypeStruct((n, d), F32),
        grid_spec=pltpu.PrefetchScalarGridSpec(
            num_scalar_prefetch=1,
            grid=(n // tc,),
            in_specs=[pl.BlockSpec(memory_space=pl.ANY),
                      pl.BlockSpec((tc, OUT_LANES), lambda i, dest: (i, 0)),
                      pl.BlockSpec((tc, d), lambda i, dest: (i, 0)),
                      pl.BlockSpec((tc, d), lambda i, dest: (i, 0)),
                      pl.BlockSpec((None, 6, d), lambda i, dest: (i // per_b, 0, 0)),
                      pl.BlockSpec((1, d), lambda i, dest: (0, 0))],
            out_specs=pl.BlockSpec((tc, d), lambda i, dest: (i, 0)),
            scratch_shapes=[pltpu.VMEM((2, TOP_K, tc, d), F32),
                            pltpu.SemaphoreType.DMA((2,))]),
        compiler_params=_cparams(("arbitrary",), V7X_VMEM_LIMIT),
        name="combine",
    )(dest_flat, y, gates, shared, x1, mod, g)


def _t5_bucket(dist):
    max_exact = REL_BUCKETS // 2
    d_f = jnp.maximum(dist, max_exact).astype(F32)
    large = max_exact + (jnp.log(d_f / max_exact) / math.log(REL_MAX_DISTANCE / max_exact)
                         * (REL_BUCKETS - max_exact)).astype(jnp.int32)
    large = jnp.minimum(large, REL_BUCKETS - 1)
    return jnp.where(dist < max_exact, dist, large)


def _bucket_tables():
    i = jnp.arange(DSWA_BLK)[:, None]
    j = jnp.arange(2 * DSWA_BLK)[None, :]
    diff = jnp.maximum(i + DSWA_BLK - j, 0)
    return jnp.stack([_t5_bucket(diff * dil) for _, dil in DSWA_PATTERNS]).astype(jnp.int32)


def _strict_lower(n):
    return jnp.arange(n)[:, None] > jnp.arange(n)[None, :]


def _layer(x, c_pad, rel_bias, norm_pre_attn, norm_post_attn, norm_pre_ffn, norm_post_ffn,
           w_ada, b_ada, w_in, head_gain_dswa, head_gain_sb, w_out, w_router, router_bias,
           w_gate_e, w_up_e, w_down_e, w_gate_s, w_up_s, w_down_s):
    bsz, seq, d = x.shape
    n = bsz * seq
    n_heads = d // (2 * HEAD_DIM)
    n_exp = w_router.shape[1]
    row = lambda v: v.reshape(1, -1)

    mod = _ada(c_pad, w_ada, row(b_ada))[:bsz].reshape(bsz, 6, d)
    x2 = x.reshape(n, d)

    w_in_bf = w_in.astype(BF16)
    qkvq = _qkv(x2, mod, row(norm_pre_attn), w_in_bf, seq, 0, 2 * d, F32).reshape(bsz, seq, 2 * d)
    kv_sb = _qkv(x2, mod, row(norm_pre_attn), w_in_bf, seq, 2 * d, 3 * d, BF16).reshape(bsz, seq, d)

    oa = _dswa(rel_bias.T, qkvq, _bucket_tables(), row(head_gain_dswa), n_heads)
    tq = min(SB_TILE, seq)
    neg_tri = jnp.where(_strict_lower(tq), -1.0, 0.0).astype(BF16)
    ob = _sb(qkvq, kv_sb, jnp.concatenate([neg_tri, neg_tri], axis=0), row(head_gain_sb), n_heads,
             3 * n_heads)

    wr_hi = w_router.astype(BF16)
    wr_lo = (w_router - wr_hi.astype(F32)).astype(BF16)
    x1, h2p, logits = _outproj(oa.reshape(n, -1), ob.reshape(n, -1), w_out.astype(BF16), x2, mod,
                                   row(norm_post_attn), row(norm_pre_ffn), wr_hi, wr_lo, seq)

    tr = min(ROUTE_TILE, n)
    counts = _route_count(logits, row(router_bias), tr)[0].astype(jnp.int32)
    padded = (counts + MOE_TILE - 1) // MOE_TILE * MOE_TILE
    pad_end = jnp.cumsum(padded)
    pad_start = pad_end - padded
    n_rows = (n * TOP_K + n_exp * (MOE_TILE - 1) + MOE_TILE - 1) // MOE_TILE * MOE_TILE
    n_tiles = n_rows // MOE_TILE
    n_used = jnp.maximum(pad_end[-1] // MOE_TILE, 1).astype(jnp.int32)
    tile_id = jnp.minimum(jnp.arange(n_tiles, dtype=jnp.int32), n_used - 1)
    tile_expert = jnp.minimum(
        jnp.sum(pad_end[None, :] <= (tile_id * MOE_TILE)[:, None], axis=1), n_exp - 1).astype(jnp.int32)

    n_units = (2 * d + w_gate_e.shape[2]) // MOE_CHUNK_ROWS
    nonempty = padded > 0
    ids = jnp.arange(n_exp, dtype=jnp.int32)
    later = lax.cummin(jnp.where(nonempty, ids, n_exp)[::-1])[::-1]
    next_e = jnp.concatenate([later[1:], jnp.full((1,), n_exp, jnp.int32)])
    order = jnp.cumsum(nonempty.astype(jnp.int32)) - 1
    mine = tile_expert[:, None] == ids[None, :]
    per_tile = lambda v: jnp.sum(jnp.where(mine, v[None, :], 0), axis=1)
    t_in = tile_id - per_tile(pad_start // MOE_TILE)
    t_cnt = jnp.maximum(per_tile(padded // MOE_TILE), 1)
    nxt = per_tile(next_e)
    live = (jnp.arange(n_tiles) < n_used) & (nxt < n_exp)
    unit_lo = jnp.where(live, n_units * t_in // t_cnt, 0).astype(jnp.int32)
    unit_hi = jnp.where(live, n_units * (t_in + 1) // t_cnt, 0).astype(jnp.int32)
    tile_next = jnp.minimum(nxt, n_exp - 1).astype(jnp.int32)
    tile_slot = (per_tile(order) % 2).astype(jnp.int32)

    dest, gates = _route_assign(logits, row(router_bias), _strict_lower(tr).astype(BF16),
                                row(pad_start.astype(F32)))
    dest_flat = dest[:, :TOP_K].reshape(-1)

    shared, xs = _dispatch_shared(dest_flat, (pad_start + counts).astype(jnp.int32),
                                  pad_end.astype(jnp.int32), h2p, n_rows, w_gate_s.astype(BF16),
                                  w_up_s.astype(BF16), w_down_s.astype(BF16))
    y = _moe(tile_expert, n_used.reshape(1), tile_slot, tile_next, unit_lo, unit_hi, xs,
             w_gate_e, w_up_e, w_down_e)
    out = _combine(dest_flat, y, gates, shared, x1, mod, row(norm_post_ffn), seq)
    return out.reshape(bsz, seq, d)


def kernel(x, c, rel_bias, norm_pre_attn, norm_post_attn, norm_pre_ffn, norm_post_ffn, w_ada, b_ada, w_in, head_gain_dswa, head_gain_sb, w_out, w_router, router_bias, w_gate_e, w_up_e, w_down_e, w_gate_s, w_up_s, w_down_s):
    bsz = x.shape[0]
    depth = w_ada.shape[0]
    c_pad = jnp.zeros((8, c.shape[1]), F32).at[:bsz].set(c)
    for l in range(depth):
        x = _layer(x, c_pad, rel_bias, norm_pre_attn[l], norm_post_attn[l], norm_pre_ffn[l],
                   norm_post_ffn[l], w_ada[l], b_ada[l], w_in[l], head_gain_dswa[l], head_gain_sb[l],
                   w_out[l], w_router[l], router_bias[l], w_gate_e[l], w_up_e[l], w_down_e[l],
                   w_gate_s[l], w_up_s[l], w_down_s[l])
    return x
```

```python
import math

import jax
import jax.numpy as jnp
from jax import lax
from jax.experimental import pallas as pl
from jax.experimental.pallas import tpu as pltpu

F32 = jnp.float32
BF16 = jnp.bfloat16

HEAD_DIM = 128
DSWA_PATTERNS = ((128, 1), (512, 4), (2048, 16))
DSWA_BLK = 128
DSWA_GROUP = 4
REL_BUCKETS = 32
REL_MAX_DISTANCE = 2048
TOP_K = 6
ROUTED_SCALING = 2.446
NORM_EPS = 1e-6
NEG_BIG = -1e30
LOG2E = 1.4426950408889634

V7X_VMEM_LIMIT = 56 * 1024 * 1024
V7X_VMEM_LIMIT_MOE = 60 * 1024 * 1024
MOE_TILE = 256
MOE_CHUNK_ROWS = 128
MOE_RING = 8
SB_TILE = 256
SB_HEADS = 4
SB_ROW_SPLIT = 2
SB_UNDERFLOW_LOG2 = -160.0
ROUTE_TILE = 512
DISPATCH_TILE = 256
COMBINE_TILE = 128
OUT_LANES = 128
LANES = 128


def _cparams(sem, vmem=None):
    return pltpu.CompilerParams(dimension_semantics=sem, vmem_limit_bytes=vmem)


def _rms(x, g):
    return x * lax.rsqrt(jnp.mean(x * x, axis=-1, keepdims=True) + NORM_EPS) * g


def _ada_kernel(c_ref, w_ref, b_ref, o_ref):
    c = c_ref[...]
    ca = c / (1.0 + jnp.exp(-c))
    o_ref[...] = jnp.dot(ca.astype(BF16), w_ref[...].astype(BF16),
                         preferred_element_type=F32) + b_ref[...]


def _ada(c_pad, w_ada, b_ada):
    rows, d = c_pad.shape
    n = w_ada.shape[1]
    tn = min(n, 1536)
    return pl.pallas_call(
        _ada_kernel,
        out_shape=jax.ShapeDtypeStruct((rows, n), F32),
        grid=(n // tn,),
        in_specs=[pl.BlockSpec((rows, d), lambda j: (0, 0)),
                  pl.BlockSpec((d, tn), lambda j: (0, j)),
                  pl.BlockSpec((1, tn), lambda j: (0, j))],
        out_specs=pl.BlockSpec((rows, tn), lambda j: (0, j)),
        compiler_params=_cparams(("arbitrary",), V7X_VMEM_LIMIT),
        name="ada",
    )(c_pad, w_ada, b_ada)


def _qkv_kernel(x_ref, mod_ref, g_ref, w_ref, o_ref, h_ref):
    @pl.when(pl.program_id(1) == 0)
    def _():
        h = _rms(x_ref[...], g_ref[...]) * (1.0 + mod_ref[1:2, :]) + mod_ref[0:1, :]
        h_ref[...] = h.astype(BF16)

    o_ref[...] = jnp.dot(h_ref[...], w_ref[...], preferred_element_type=F32).astype(o_ref.dtype)


def _qkv(x2, mod, g, w_bf, seq, col_lo, col_hi, out_dtype):
    n, d = x2.shape
    nout = col_hi - col_lo
    tm = min(1024, seq)
    tn = min(1024, nout)
    j0 = col_lo // tn
    per_b = seq // tm
    return pl.pallas_call(
        _qkv_kernel,
        out_shape=jax.ShapeDtypeStruct((n, nout), out_dtype),
        grid=(n // tm, nout // tn),
        in_specs=[pl.BlockSpec((tm, d), lambda i, j: (i, 0)),
                  pl.BlockSpec((None, 6, d), lambda i, j: (i // per_b, 0, 0)),
                  pl.BlockSpec((1, d), lambda i, j: (0, 0)),
                  pl.BlockSpec((d, tn), lambda i, j: (0, j0 + j))],
        out_specs=pl.BlockSpec((tm, tn), lambda i, j: (i, j)),
        scratch_shapes=[pltpu.VMEM((tm, d), BF16)],
        compiler_params=_cparams(("parallel", "arbitrary"), V7X_VMEM_LIMIT),
        name="qkv",
    )(x2, mod, g, w_bf)


def _dswa_kernel(rb_ref, q_ref, k_ref, v_ref, bucket_ref, g_ref, o_ref, *stats):
    n_pat = len(DSWA_PATTERNS)
    acc_refs, m_refs, l_refs = stats[:n_pat], stats[n_pat:2 * n_pat], stats[2 * n_pat:]
    seq = q_ref.shape[0]
    head = pl.program_id(1)
    scale = 1.0 / math.sqrt(HEAD_DIM)
    blk = DSWA_BLK
    ii = lax.broadcasted_iota(jnp.int32, (blk, 2 * blk), 0)
    jj = lax.broadcasted_iota(jnp.int32, (blk, 2 * blk), 1)
    diff = ii + blk - jj
    band = (diff >= 0) & (diff <= blk)

    for p, (_, dil) in enumerate(DSWA_PATTERNS):
        nb = seq // dil // blk
        bucket = bucket_ref[p]
        bias = jnp.zeros((blk, 2 * blk), F32)
        for b in range(REL_BUCKETS):
            bias = jnp.where(bucket == b, rb_ref[head, b], bias)
        bias = jnp.where(band, bias, NEG_BIG)
        acc_ref, m_ref, l_ref = acc_refs[p], m_refs[p], l_refs[p]

        def rows(start, size, dil=dil):
            if dil == 1:
                return pl.ds(start, size)
            return pl.ds(start, size, stride=dil)

        def group(blocks, bias=bias, rows=rows, dil=dil,
                  acc_ref=acc_ref, m_ref=m_ref, l_ref=l_ref):
            rqs, rks, bss = [], [], []
            for r, n in blocks:
                rq = rows(n * blk * dil + r, blk)
                rqs.append(rq)
                rks.append(rq if n == 0 else rows((n - 1) * blk * dil + r, 2 * blk))
                bss.append(bias[:, blk:] if n == 0 else bias)
            qs = [(q_ref[rq, :] * scale).astype(BF16) for rq in rqs]
            ks = [k_ref[rk, :].astype(BF16) for rk in rks]
            ss = [lax.dot_general(q, k, (((1,), (1,)), ((), ())), preferred_element_type=F32) + bs
                  for q, k, bs in zip(qs, ks, bss)]
            ms = [jnp.max(s, axis=-1, keepdims=True) for s in ss]
            ps = [jnp.exp(s - m) for s, m in zip(ss, ms)]
            vs = [v_ref[rk, :].astype(BF16) for rk in rks]
            for rq, m, pexp, v in zip(rqs, ms, ps, vs):
                acc_ref[rq, :] = jnp.dot(pexp.astype(BF16), v, preferred_element_type=F32)
                m_ref[rq, :] = jnp.broadcast_to(m, (blk, HEAD_DIM))
                l_ref[rq, :] = jnp.broadcast_to(jnp.sum(pexp, axis=-1, keepdims=True),
                                                (blk, HEAD_DIM))

        per_phase = min(nb, DSWA_GROUP)
        phases = max(1, min(dil, DSWA_GROUP // per_phase))

        def trip(t, carry, nb=nb, per_phase=per_phase, phases=phases, group=group):
            for n0 in range(0, nb, per_phase):
                group([(t * phases + dr, n0 + dn) for dr in range(phases) for dn in range(per_phase)])
            return carry

        lax.fori_loop(0, dil // phases, trip, 0)

    chunk = min(256, seq)

    def finish(c, carry):
        rs = pl.ds(pl.multiple_of(c * chunk, chunk), chunk)
        ms = [m_ref[rs, :] for m_ref in m_refs]
        m_all = ms[0]
        for m in ms[1:]:
            m_all = jnp.maximum(m_all, m)
        num = jnp.zeros((chunk, HEAD_DIM), F32)
        den = jnp.zeros((chunk, HEAD_DIM), F32)
        for p in range(n_pat):
            w = jnp.exp(ms[p] - m_all)
            num = num + w * acc_refs[p][rs, :]
            den = den + w * l_refs[p][rs, :]
        o_ref[rs, :] = _rms(num / den, g_ref[...]).astype(o_ref.dtype)
        return carry

    lax.fori_loop(0, seq // chunk, finish, 0)


def _dswa(rel_bias_t, qkv3, buckets, gain, n_heads):
    bsz, seq, _ = qkv3.shape
    w = n_heads * HEAD_DIM
    return pl.pallas_call(
        _dswa_kernel,
        out_shape=jax.ShapeDtypeStruct((bsz, seq, w), BF16),
        grid=(bsz, n_heads),
        in_specs=[pl.BlockSpec(memory_space=pltpu.SMEM),
                  pl.BlockSpec((None, seq, HEAD_DIM), lambda b, h: (b, 0, h)),
                  pl.BlockSpec((None, seq, HEAD_DIM), lambda b, h: (b, 0, n_heads + h)),
                  pl.BlockSpec((None, seq, HEAD_DIM), lambda b, h: (b, 0, 2 * n_heads + h)),
                  pl.BlockSpec(buckets.shape, lambda b, h: (0, 0, 0)),
                  pl.BlockSpec((1, HEAD_DIM), lambda b, h: (0, h))],
        out_specs=pl.BlockSpec((None, seq, HEAD_DIM), lambda b, h: (b, 0, h)),
        scratch_shapes=[pltpu.VMEM((seq, HEAD_DIM), F32)] * (3 * len(DSWA_PATTERNS)),
        compiler_params=_cparams(("parallel", "parallel"), V7X_VMEM_LIMIT),
        name="dswa",
    )(rel_bias_t, qkv3, qkv3, qkv3, buckets, gain)


def _neg_abs(z):
    bits = lax.bitcast_convert_type(z, jnp.uint32) | jnp.uint32(0x80000000)
    return lax.bitcast_convert_type(bits, F32)


def _sb_kernel(q_ref, kb_ref, vb_ref, tri_ref, g_ref, o_ref):
    tq = q_ref.shape[0]
    nh = q_ref.shape[1] // HEAD_DIM
    iq = pl.program_id(2)

    q_all = (q_ref[...] * (LOG2E / math.sqrt(HEAD_DIM))).astype(BF16)
    rs = tq // SB_ROW_SPLIT
    ti = lax.broadcasted_iota(jnp.int32, (tq, tq), 0)
    si = lax.broadcasted_iota(jnp.int32, (tq, tq), 1)
    causal = si < ti
    chains = [(h, s) for h in range(nh) for s in range(SB_ROW_SPLIT)]
    cols = [slice(h * HEAD_DIM, (h + 1) * HEAD_DIM) for h, _ in chains]
    rws = [slice(s * rs, (s + 1) * rs) for _, s in chains]
    qs = [q_all[r, c] for r, c in zip(rws, cols)]
    masks = [causal[r, :] for r in rws]

    def visit(jk, state, masked):
        r0 = pl.multiple_of(jk * tq, tq)
        carries, accs = state
        ks = [kb_ref[pl.ds(r0, tq), c] for c in cols]
        vs = [vb_ref[pl.ds(r0, tq), c] for c in cols]
        zs = [lax.dot_general(q, k, (((1,), (1,)), ((), ())), preferred_element_type=F32)
              for q, k in zip(qs, ks)]
        his, los, logsig, rowsums = [], [], [], []
        for z, m in zip(zs, masks):
            sp = jnp.maximum(z, 0.0) + jnp.log(1.0 + jnp.exp2(_neg_abs(z))) * LOG2E
            if masked:
                sp = jnp.where(m, sp, 0.0)
            hi = sp.astype(BF16)
            his.append(hi)
            los.append((sp - hi.astype(F32)).astype(BF16))
            logsig.append(z - sp)
            rowsums.append(jnp.sum(sp, axis=-1, keepdims=True))
        within = [jnp.dot(jnp.concatenate([hi, lo], axis=1), tri_ref[...], preferred_element_type=F32)
                  for hi, lo in zip(his, los)]
        probs = [jnp.exp2(ls + wi + carry) for ls, wi, carry in zip(logsig, within, carries)]
        if masked:
            probs = [jnp.where(m, a, 0.0) for m, a in zip(masks, probs)]
        accs = tuple(acc + jnp.dot(a.astype(BF16), v, preferred_element_type=F32)
                     for acc, a, v in zip(accs, probs, vs))
        carries = tuple(carry - rsum for carry, rsum in zip(carries, rowsums))
        return carries, accs

    def remaining(carries):
        left = carries[0]
        for c in carries[1:]:
            left = jnp.maximum(left, c)
        return jnp.max(left)

    state = (tuple(jnp.zeros((rs, 1), F32) for _ in chains),
             tuple(jnp.zeros((rs, HEAD_DIM), F32) for _ in chains))
    state = visit(iq, state, True)

    def more(loop):
        t, left, _ = loop
        return (t < iq) & (left > SB_UNDERFLOW_LOG2)

    def step(loop):
        t, _, st = loop
        st = visit(iq - 1 - t, st, False)
        return t + 1, remaining(st[0]), st

    _, _, state = lax.while_loop(more, step, (jnp.int32(0), remaining(state[0]), state))
    for r, c, acc in zip(rws, cols, state[1]):
        o_ref[r, c] = _rms(acc, g_ref[:, c]).astype(o_ref.dtype)


def _sb(q_src, kv, tri2, gain, n_heads, q_col0):
    bsz, seq, _ = kv.shape
    tq = min(SB_TILE, seq)
    nh = min(SB_HEADS, n_heads)
    wblk = nh * HEAD_DIM
    groups = n_heads // nh
    c0 = q_col0 // nh
    return pl.pallas_call(
        _sb_kernel,
        out_shape=jax.ShapeDtypeStruct((bsz, seq, n_heads * HEAD_DIM), BF16),
        grid=(bsz, groups, seq // tq),
        in_specs=[pl.BlockSpec((None, tq, wblk), lambda b, h, i: (b, i, c0 + h)),
                  pl.BlockSpec((None, seq, wblk), lambda b, h, i: (b, 0, h)),
                  pl.BlockSpec((None, seq, wblk), lambda b, h, i: (b, 0, groups + h)),
                  pl.BlockSpec(tri2.shape, lambda b, h, i: (0, 0)),
                  pl.BlockSpec((1, wblk), lambda b, h, i: (0, h))],
        out_specs=pl.BlockSpec((None, tq, wblk), lambda b, h, i: (b, i, h)),
        compiler_params=_cparams(("parallel", "parallel", "arbitrary"), V7X_VMEM_LIMIT),
        name="sb",
    )(q_src, kv, kv, tri2, gain)


def _pack_halves(v):
    c = v.shape[1] // 2
    bits = lax.bitcast_convert_type(v, jnp.uint32)
    return (bits[:, :c] >> 16) | (bits[:, c:] & jnp.uint32(0xFFFF0000))


def _unpack_halves(p):
    lo = lax.bitcast_convert_type(p << 16, F32).astype(BF16)
    hi = lax.bitcast_convert_type(p & jnp.uint32(0xFFFF0000), F32).astype(BF16)
    return jnp.concatenate([lo, hi], axis=1)


def _store_rows(ref, v):
    rows, width = v.shape
    c = width // LANES
    for j in range(c):
        ref[pl.ds(j, rows, stride=c), :] = v[:, j * LANES:(j + 1) * LANES]


def _load_rows(ref, rows):
    c = ref.shape[0] // rows
    return jnp.concatenate([ref[pl.ds(j, rows, stride=c), :] for j in range(c)], axis=1)


def _outproj_kernel(oa_ref, ob_ref, w_ref, x_ref, mod_ref, gpa_ref, gpf_ref, wr_hi_ref, wr_lo_ref,
                    x1_ref, h2p_ref, lg_ref):
    wa = oa_ref.shape[1]
    o = jnp.dot(oa_ref[...], w_ref[0:wa, :], preferred_element_type=F32)
    o = o + jnp.dot(ob_ref[...], w_ref[wa:, :], preferred_element_type=F32)
    x1 = x_ref[...] + mod_ref[2:3, :] * _rms(o, gpa_ref[...])
    x1_ref[...] = x1
    h2 = _rms(x1, gpf_ref[...]) * (1.0 + mod_ref[4:5, :]) + mod_ref[3:4, :]
    hi = h2.astype(BF16)
    hi_f = hi.astype(F32)
    _store_rows(h2p_ref, _pack_halves(hi_f))
    lo = (h2 - hi_f).astype(BF16)
    lg = jnp.dot(hi, wr_hi_ref[...], preferred_element_type=F32)
    lg = lg + jnp.dot(lo, wr_hi_ref[...], preferred_element_type=F32)
    lg = lg + jnp.dot(hi, wr_lo_ref[...], preferred_element_type=F32)
    lg_ref[...] = lg


def _outproj(oa, ob, w_bf, x2, mod, gpa, gpf, wr_hi, wr_lo, seq):
    n, d = x2.shape
    wa = oa.shape[1]
    ne = wr_hi.shape[1]
    tm = min(256, seq)
    per_b = seq // tm
    const = lambda i: (0, 0)
    return pl.pallas_call(
        _outproj_kernel,
        out_shape=(jax.ShapeDtypeStruct((n, d), F32),
                   jax.ShapeDtypeStruct((n * (d // 2 // LANES), LANES), jnp.uint32),
                   jax.ShapeDtypeStruct((n, ne), F32)),
        grid=(n // tm,),
        in_specs=[pl.BlockSpec((tm, wa), lambda i: (i, 0)),
                  pl.BlockSpec((tm, wa), lambda i: (i, 0)),
                  pl.BlockSpec(w_bf.shape, const),
                  pl.BlockSpec((tm, d), lambda i: (i, 0)),
                  pl.BlockSpec((None, 6, d), lambda i: (i // per_b, 0, 0)),
                  pl.BlockSpec((1, d), const),
                  pl.BlockSpec((1, d), const),
                  pl.BlockSpec(wr_hi.shape, const),
                  pl.BlockSpec(wr_lo.shape, const)],
        out_specs=(pl.BlockSpec((tm, d), lambda i: (i, 0)),
                   pl.BlockSpec((tm * (d // 2 // LANES), LANES), lambda i: (i, 0)),
                   pl.BlockSpec((tm, ne), lambda i: (i, 0))),
        compiler_params=_cparams(("parallel",), V7X_VMEM_LIMIT),
        name="outproj",
    )(oa, ob, w_bf, x2, mod, gpa, gpf, wr_hi, wr_lo)


def _topk(lg, rbias):
    tr, ne = lg.shape
    scores = 1.0 / (1.0 + jnp.exp(-lg))
    sel = scores + rbias
    lane = lax.broadcasted_iota(jnp.int32, (tr, ne), 1).astype(F32)
    onehots, gates = [], []
    chosen = jnp.zeros((tr, ne), F32)
    for _ in range(TOP_K):
        m = jnp.max(sel, axis=-1, keepdims=True)
        idx = jnp.min(jnp.where(sel == m, lane, float(ne)), axis=-1, keepdims=True)
        oh = lane == idx
        gates.append(jnp.sum(jnp.where(oh, scores, 0.0), axis=-1, keepdims=True))
        sel = jnp.where(oh, -jnp.inf, sel)
        chosen = jnp.where(oh, 1.0, chosen)
        onehots.append(oh)
    return onehots, gates, chosen


def _route_count_kernel(lg_ref, rbias_ref, cnt_ref):
    @pl.when(pl.program_id(0) == 0)
    def _():
        cnt_ref[...] = jnp.zeros_like(cnt_ref)

    _, _, chosen = _topk(lg_ref[...], rbias_ref[...])
    cnt_ref[0:1, :] = cnt_ref[0:1, :] + jnp.sum(chosen, axis=0, keepdims=True)


def _route_count(logits, rbias, tr):
    n, ne = logits.shape
    return pl.pallas_call(
        _route_count_kernel,
        out_shape=jax.ShapeDtypeStruct((8, ne), F32),
        grid=(n // tr,),
        in_specs=[pl.BlockSpec((tr, ne), lambda i: (i, 0)),
                  pl.BlockSpec((1, ne), lambda i: (0, 0))],
        out_specs=pl.BlockSpec((8, ne), lambda i: (0, 0)),
        compiler_params=_cparams(("arbitrary",)),
        name="route_count",
    )(logits, rbias)


def _route_assign_kernel(lg_ref, rbias_ref, tri_ref, start_ref, dest_ref, gate_ref, seen_ref):
    tr = lg_ref.shape[0]

    @pl.when(pl.program_id(0) == 0)
    def _():
        seen_ref[...] = jnp.zeros_like(seen_ref)

    onehots, gates, chosen = _topk(lg_ref[...], rbias_ref[...])
    gsum = gates[0]
    for g in gates[1:]:
        gsum = gsum + g
    before = jnp.dot(tri_ref[...], chosen.astype(BF16), preferred_element_type=F32)
    slot = before + seen_ref[0:1, :] + start_ref[...]
    seen_ref[0:1, :] = seen_ref[0:1, :] + jnp.sum(chosen, axis=0, keepdims=True)

    out_lane = lax.broadcasted_iota(jnp.int32, (tr, OUT_LANES), 1)
    dest_o = jnp.zeros((tr, OUT_LANES), jnp.int32)
    gate_o = jnp.zeros((tr, OUT_LANES), F32)
    for k in range(TOP_K):
        dest = jnp.sum(jnp.where(onehots[k], slot, 0.0), axis=-1, keepdims=True)
        dest_o = jnp.where(out_lane == k, dest.astype(jnp.int32), dest_o)
        gate_o = jnp.where(out_lane == k, gates[k] / gsum * ROUTED_SCALING, gate_o)
    dest_ref[...] = dest_o
    gate_ref[...] = gate_o


def _route_assign(logits, rbias, tri, start):
    n, ne = logits.shape
    tr = tri.shape[0]
    return pl.pallas_call(
        _route_assign_kernel,
        out_shape=(jax.ShapeDtypeStruct((n, OUT_LANES), jnp.int32),
                   jax.ShapeDtypeStruct((n, OUT_LANES), F32)),
        grid=(n // tr,),
        in_specs=[pl.BlockSpec((tr, ne), lambda i: (i, 0)),
                  pl.BlockSpec((1, ne), lambda i: (0, 0)),
                  pl.BlockSpec((tr, tr), lambda i: (0, 0)),
                  pl.BlockSpec((1, ne), lambda i: (0, 0))],
        out_specs=(pl.BlockSpec((tr, OUT_LANES), lambda i: (i, 0)),
                   pl.BlockSpec((tr, OUT_LANES), lambda i: (i, 0))),
        scratch_shapes=[pltpu.VMEM((8, ne), F32)],
        compiler_params=_cparams(("arbitrary",)),
        name="route_assign",
    )(logits, rbias, tri, start)


def _dispatch_shared_kernel(dest_ref, lo_ref, hi_ref, h_ref, wg_ref, wu_ref, wd_ref,
                            y_ref, xs_ref, zero_ref, sem, zsem):
    td = y_ref.shape[0]
    lines = h_ref.shape[0] // td
    step = pl.program_id(0)
    n_exp = lo_ref.shape[0]
    tile = zero_ref.shape[0]

    def row(ref, r):
        return ref.at[pl.ds(pl.multiple_of(r * lines, lines), lines)]

    @pl.when(step == 0)
    def _():
        zero_ref[...] = jnp.zeros_like(zero_ref)

        def zcopy(p):
            return pltpu.make_async_copy(zero_ref.at[pl.ds(0, lines)], row(xs_ref, p), zsem.at[0])

        def fill(e, c):
            lax.fori_loop(lo_ref[e], hi_ref[e], lambda p, c2: (zcopy(p).start(), c2)[1], 0)
            lax.fori_loop(lo_ref[e], hi_ref[e], lambda p, c2: (zcopy(p).wait(), c2)[1], 0)
            return c

        lax.fori_loop(0, n_exp, fill, 0)

        def tcopy(t):
            span = pl.ds(pl.multiple_of(t * tile, tile), tile)
            return pltpu.make_async_copy(zero_ref, xs_ref.at[span], zsem.at[1])

        first, last = hi_ref[n_exp - 1] * lines // tile, xs_ref.shape[0] // tile
        lax.fori_loop(first, last, lambda t, c2: (tcopy(t).start(), c2)[1], 0)
        lax.fori_loop(first, last, lambda t, c2: (tcopy(t).wait(), c2)[1], 0)

    base = step * (td * TOP_K)

    def copy(t, k):
        return pltpu.make_async_copy(h_ref.at[pl.ds(t * lines, lines)],
                                     row(xs_ref, dest_ref[base + t * TOP_K + k]), sem.at[0])

    for t in range(td):
        for k in range(TOP_K):
            copy(t, k).start()

    x = _unpack_halves(_load_rows(h_ref, td))
    hg = jnp.dot(x, wg_ref[...], preferred_element_type=F32)
    hu = jnp.dot(x, wu_ref[...], preferred_element_type=F32)
    h = (hg / (1.0 + jnp.exp(-hg))) * hu
    y_ref[...] = jnp.dot(h.astype(BF16), wd_ref[...], preferred_element_type=F32)

    for t in range(td):
        for k in range(TOP_K):
            copy(t, k).wait()


def _dispatch_shared(dest_flat, fill_lo, fill_hi, h2p, n_rows, wg, wu, wd):
    d, f = wg.shape
    lines = d // 2 // LANES
    n = h2p.shape[0] // lines
    td = min(DISPATCH_TILE, n)
    const = lambda i, *_: (0, 0)
    return pl.pallas_call(
        _dispatch_shared_kernel,
        out_shape=(jax.ShapeDtypeStruct((n, d), F32),
                   jax.ShapeDtypeStruct((n_rows * lines, LANES), h2p.dtype)),
        grid_spec=pltpu.PrefetchScalarGridSpec(
            num_scalar_prefetch=3,
            grid=(n // td,),
            in_specs=[pl.BlockSpec((td * lines, LANES), lambda i, *_: (i, 0)),
                      pl.BlockSpec((d, f), const),
                      pl.BlockSpec((d, f), const),
                      pl.BlockSpec((f, d), const)],
            out_specs=(pl.BlockSpec((td, d), lambda i, *_: (i, 0)),
                       pl.BlockSpec(memory_space=pl.ANY)),
            scratch_shapes=[pltpu.VMEM((MOE_TILE * lines, LANES), h2p.dtype),
                            pltpu.SemaphoreType.DMA((1,)),
                            pltpu.SemaphoreType.DMA((2,))]),
        compiler_params=_cparams(("arbitrary",), V7X_VMEM_LIMIT),
        name="dispatch_shared",
    )(dest_flat, fill_lo, fill_hi, h2p, wg, wu, wd)


def _moe_kernel(te_ref, nu_ref, slot_ref, nxt_ref, ulo_ref, uhi_ref,
                x_ref, wg_hbm, wu_hbm, wd_hbm, y_ref,
                wg_bf, wu_bf, wd_bf, stage_gu, stage_d, sem):
    i = pl.program_id(0)
    d, f = wg_bf.shape[1], wg_bf.shape[2]
    ring, rg, rd = stage_gu.shape[0], stage_gu.shape[1], stage_d.shape[1]
    ahead = ring - 1
    ng, nd = d // rg, f // rd
    n_units = 2 * ng + nd
    kinds = ((0, ng, wg_hbm, wg_bf, stage_gu, rg), (ng, 2 * ng, wu_hbm, wu_bf, stage_gu, rg),
             (2 * ng, n_units, wd_hbm, wd_bf, stage_d, rd))

    def chunk_copy(kind, u, e):
        first, _, src, _, stage, rows = kind
        r0 = pl.multiple_of((u - first) * rows, rows)
        return pltpu.make_async_copy(src.at[e, pl.ds(r0, rows), :], stage.at[u % ring],
                                     sem.at[u % ring])

    def request(u, e):
        for kind in kinds:
            @pl.when((u >= kind[0]) & (u < kind[1]))
            def _(kind=kind):
                chunk_copy(kind, u, e).start()

    def receive(u, e, slot):
        for kind in kinds:
            @pl.when((u >= kind[0]) & (u < kind[1]))
            def _(kind=kind):
                first, _, _, dst, stage, rows = kind
                chunk_copy(kind, u, e).wait()
                r0 = pl.multiple_of((u - first) * rows, rows)
                dst[slot, pl.ds(r0, rows), :] = stage[u % ring].astype(BF16)

    def run_units(lo, hi, e, slot):
        def body(u, c):
            @pl.when(u == 0)
            def _():
                for j in range(ahead):
                    chunk_copy(kinds[0], j, e).start()

            @pl.when(u + ahead < n_units)
            def _():
                request(u + ahead, e)

            receive(u, e, slot)
            return c

        lax.fori_loop(lo, hi, body, 0)

    slot = slot_ref[i]

    @pl.when(i == 0)
    def _():
        run_units(0, n_units, te_ref[0], slot)

    run_units(ulo_ref[i], uhi_ref[i], nxt_ref[i], 1 - slot)

    @pl.when(i < nu_ref[0])
    def _():
        x = _unpack_halves(_load_rows(x_ref, MOE_TILE))
        hg = jnp.dot(x, wg_bf[slot], preferred_element_type=F32)
        hu = jnp.dot(x, wu_bf[slot], preferred_element_type=F32)
        h = (hg / (1.0 + jnp.exp(-hg))) * hu
        _store_rows(y_ref, jnp.dot(h.astype(BF16), wd_bf[slot], preferred_element_type=F32))

    @pl.when(i >= nu_ref[0])
    def _():
        y_ref[...] = jnp.zeros_like(y_ref)


def _moe(tile_expert, n_used, tile_slot, tile_next, unit_lo, unit_hi, xs, wg, wu, wd):
    _, d, f = wg.shape
    xl, yl = d // 2 // LANES, d // LANES
    rows = xs.shape[0] // xl
    tm = MOE_TILE
    rg = rd = MOE_CHUNK_ROWS
    ring = min(MOE_RING, d // rg + 1)
    return pl.pallas_call(
        _moe_kernel,
        out_shape=jax.ShapeDtypeStruct((rows * yl, LANES), F32),
        grid_spec=pltpu.PrefetchScalarGridSpec(
            num_scalar_prefetch=6,
            grid=(rows // tm,),
            in_specs=[pl.BlockSpec((tm * xl, LANES),
                                   lambda i, te, nu, *_: (jnp.minimum(i, nu[0] - 1), 0)),
                      pl.BlockSpec(memory_space=pl.ANY),
                      pl.BlockSpec(memory_space=pl.ANY),
                      pl.BlockSpec(memory_space=pl.ANY)],
            out_specs=pl.BlockSpec((tm * yl, LANES), lambda i, *_: (i, 0)),
            scratch_shapes=[pltpu.VMEM((2, d, f), BF16),
                            pltpu.VMEM((2, d, f), BF16),
                            pltpu.VMEM((2, f, d), BF16),
                            pltpu.VMEM((ring, rg, f), F32),
                            pltpu.VMEM((ring, rd, d), F32),
                            pltpu.SemaphoreType.DMA((ring,))]),
        compiler_params=_cparams(("arbitrary",), V7X_VMEM_LIMIT_MOE),
        name="moe",
    )(tile_expert, n_used, tile_slot, tile_next, unit_lo, unit_hi, xs, wg, wu, wd)


def _combine_kernel(dest_ref, y_ref, gate_ref, sh_ref, x1_ref, mod_ref, g_ref, o_ref, buf_ref, sem):
    tc = x1_ref.shape[0]
    lines = buf_ref.shape[2] // tc
    step = pl.program_id(0)
    n_steps = pl.num_programs(0)

    def copy(tile, slot, t, k):
        src = dest_ref[(tile * tc + t) * TOP_K + k]
        return pltpu.make_async_copy(
            y_ref.at[pl.ds(pl.multiple_of(src * lines, lines), lines)],
            buf_ref.at[slot, k, pl.ds(pl.multiple_of(t * lines, lines), lines)], sem.at[slot])

    def request(tile, slot):
        def body(t, c):
            for k in range(TOP_K):
                copy(tile, slot, t, k).start()
            return c
        lax.fori_loop(0, tc, body, 0)

    slot = step % 2

    @pl.when(step == 0)
    def _():
        request(step, slot)

    @pl.when(step + 1 < n_steps)
    def _():
        request(step + 1, 1 - slot)

    def await_(t, c):
        for k in range(TOP_K):
            copy(step, slot, t, k).wait()
        return c

    lax.fori_loop(0, tc, await_, 0)

    gate = gate_ref[...]
    y = sh_ref[...]
    for k in range(TOP_K):
        y = y + gate[:, k:k + 1] * _load_rows(buf_ref.at[slot, k], tc)
    o_ref[...] = x1_ref[...] + mod_ref[5:6, :] * _rms(y, g_ref[...])


def _combine(dest_flat, y, gates, shared, x1, mod, g, seq):
    n, d = x1.shape
    tc = min(COMBINE_TILE, seq)
    per_b = seq // tc
    return pl.pallas_call(
        _combine_kernel,
        out_shape=jax.ShapeDtypeStruct((n, d), F32),
        grid_spec=pltpu.PrefetchScalarGridSpec(
            num_scalar_prefetch=1,
            grid=(n // tc,),
            in_specs=[pl.BlockSpec(memory_space=pl.ANY),
                      pl.BlockSpec((tc, OUT_LANES), lambda i, dest: (i, 0)),
                      pl.BlockSpec((tc, d), lambda i, dest: (i, 0)),
                      pl.BlockSpec((tc, d), lambda i, dest: (i, 0)),
                      pl.BlockSpec((None, 6, d), lambda i, dest: (i // per_b, 0, 0)),
                      pl.BlockSpec((1, d), lambda i, dest: (0, 0))],
            out_specs=pl.BlockSpec((tc, d), lambda i, dest: (i, 0)),
            scratch_shapes=[pltpu.VMEM((2, TOP_K, tc * (d // LANES), LANES), F32),
                            pltpu.SemaphoreType.DMA((2,))]),
        compiler_params=_cparams(("arbitrary",), V7X_VMEM_LIMIT),
        name="combine",
    )(dest_flat, y, gates, shared, x1, mod, g)


def _t5_bucket(dist):
    max_exact = REL_BUCKETS // 2
    d_f = jnp.maximum(dist, max_exact).astype(F32)
    large = max_exact + (jnp.log(d_f / max_exact) / math.log(REL_MAX_DISTANCE / max_exact)
                         * (REL_BUCKETS - max_exact)).astype(jnp.int32)
    large = jnp.minimum(large, REL_BUCKETS - 1)
    return jnp.where(dist < max_exact, dist, large)


def _bucket_tables():
    i = jnp.arange(DSWA_BLK)[:, None]
    j = jnp.arange(2 * DSWA_BLK)[None, :]
    diff = jnp.maximum(i + DSWA_BLK - j, 0)
    return jnp.stack([_t5_bucket(diff * dil) for _, dil in DSWA_PATTERNS]).astype(jnp.int32)


def _strict_lower(n):
    return jnp.arange(n)[:, None] > jnp.arange(n)[None, :]


def _layer(x, c_pad, rel_bias, norm_pre_attn, norm_post_attn, norm_pre_ffn, norm_post_ffn,
           w_ada, b_ada, w_in, head_gain_dswa, head_gain_sb, w_out, w_router, router_bias,
           w_gate_e, w_up_e, w_down_e, w_gate_s, w_up_s, w_down_s):
    bsz, seq, d = x.shape
    n = bsz * seq
    n_heads = d // (2 * HEAD_DIM)
    n_exp = w_router.shape[1]
    row = lambda v: v.reshape(1, -1)

    mod = _ada(c_pad, w_ada, row(b_ada))[:bsz].reshape(bsz, 6, d)
    x2 = x.reshape(n, d)

    w_in_bf = w_in.astype(BF16)
    qkvq = _qkv(x2, mod, row(norm_pre_attn), w_in_bf, seq, 0, 2 * d, F32).reshape(bsz, seq, 2 * d)
    kv_sb = _qkv(x2, mod, row(norm_pre_attn), w_in_bf, seq, 2 * d, 3 * d, BF16).reshape(bsz, seq, d)

    oa = _dswa(rel_bias.T, qkvq, _bucket_tables(), row(head_gain_dswa), n_heads)
    tq = min(SB_TILE, seq)
    neg_tri = jnp.where(_strict_lower(tq), -1.0, 0.0).astype(BF16)
    ob = _sb(qkvq, kv_sb, jnp.concatenate([neg_tri, neg_tri], axis=0), row(head_gain_sb), n_heads,
             3 * n_heads)

    wr_hi = w_router.astype(BF16)
    wr_lo = (w_router - wr_hi.astype(F32)).astype(BF16)
    x1, h2p, logits = _outproj(oa.reshape(n, -1), ob.reshape(n, -1), w_out.astype(BF16), x2, mod,
                                   row(norm_post_attn), row(norm_pre_ffn), wr_hi, wr_lo, seq)

    tr = min(ROUTE_TILE, n)
    counts = _route_count(logits, row(router_bias), tr)[0].astype(jnp.int32)
    padded = (counts + MOE_TILE - 1) // MOE_TILE * MOE_TILE
    pad_end = jnp.cumsum(padded)
    pad_start = pad_end - padded
    n_rows = (n * TOP_K + n_exp * (MOE_TILE - 1) + MOE_TILE - 1) // MOE_TILE * MOE_TILE
    n_tiles = n_rows // MOE_TILE
    n_used = jnp.maximum(pad_end[-1] // MOE_TILE, 1).astype(jnp.int32)
    tile_id = jnp.minimum(jnp.arange(n_tiles, dtype=jnp.int32), n_used - 1)
    tile_expert = jnp.minimum(
        jnp.sum(pad_end[None, :] <= (tile_id * MOE_TILE)[:, None], axis=1), n_exp - 1).astype(jnp.int32)

    n_units = (2 * d + w_gate_e.shape[2]) // MOE_CHUNK_ROWS
    nonempty = padded > 0
    ids = jnp.arange(n_exp, dtype=jnp.int32)
    later = lax.cummin(jnp.where(nonempty, ids, n_exp)[::-1])[::-1]
    next_e = jnp.concatenate([later[1:], jnp.full((1,), n_exp, jnp.int32)])
    order = jnp.cumsum(nonempty.astype(jnp.int32)) - 1
    mine = tile_expert[:, None] == ids[None, :]
    per_tile = lambda v: jnp.sum(jnp.where(mine, v[None, :], 0), axis=1)
    t_in = tile_id - per_tile(pad_start // MOE_TILE)
    t_cnt = jnp.maximum(per_tile(padded // MOE_TILE), 1)
    nxt = per_tile(next_e)
    live = (jnp.arange(n_tiles) < n_used) & (nxt < n_exp)
    unit_lo = jnp.where(live, n_units * t_in // t_cnt, 0).astype(jnp.int32)
    unit_hi = jnp.where(live, n_units * (t_in + 1) // t_cnt, 0).astype(jnp.int32)
    tile_next = jnp.minimum(nxt, n_exp - 1).astype(jnp.int32)
    tile_slot = (per_tile(order) % 2).astype(jnp.int32)

    dest, gates = _route_assign(logits, row(router_bias), _strict_lower(tr).astype(BF16),
                                row(pad_start.astype(F32)))
    dest_flat = dest[:, :TOP_K].reshape(-1)

    shared, xs = _dispatch_shared(dest_flat, (pad_start + counts).astype(jnp.int32),
                                  pad_end.astype(jnp.int32), h2p, n_rows, w_gate_s.astype(BF16),
                                  w_up_s.astype(BF16), w_down_s.astype(BF16))
    y = _moe(tile_expert, n_used.reshape(1), tile_slot, tile_next, unit_lo, unit_hi, xs,
             w_gate_e, w_up_e, w_down_e)
    out = _combine(dest_flat, y, gates, shared, x1, mod, row(norm_post_ffn), seq)
    return out.reshape(bsz, seq, d)


def kernel(x, c, rel_bias, norm_pre_attn, norm_post_attn, norm_pre_ffn, norm_post_ffn, w_ada, b_ada, w_in, head_gain_dswa, head_gain_sb, w_out, w_router, router_bias, w_gate_e, w_up_e, w_down_e, w_gate_s, w_up_s, w_down_s):
    bsz = x.shape[0]
    depth = w_ada.shape[0]
    c_pad = jnp.zeros((8, c.shape[1]), F32).at[:bsz].set(c)
    for l in range(depth):
        x = _layer(x, c_pad, rel_bias, norm_pre_attn[l], norm_post_attn[l], norm_pre_ffn[l],
                   norm_post_ffn[l], w_ada[l], b_ada[l], w_in[l], head_gain_dswa[l], head_gain_sb[l],
                   w_out[l], w_router[l], router_bias[l], w_gate_e[l], w_up_e[l], w_down_e[l],
                   w_gate_s[l], w_up_s[l], w_down_s[l])
    return x
```

```python
import math

import jax
import jax.numpy as jnp
from jax import lax
from jax.experimental import pallas as pl
from jax.experimental.pallas import tpu as pltpu

F32 = jnp.float32
BF16 = jnp.bfloat16

HEAD_DIM = 128
DSWA_PATTERNS = ((128, 1), (512, 4), (2048, 16))
DSWA_BLK = 128
DSWA_GROUP = 8
REL_BUCKETS = 32
REL_MAX_DISTANCE = 2048
TOP_K = 6
ROUTED_SCALING = 2.446
NORM_EPS = 1e-6
NEG_BIG = -1e30
LOG2E = 1.4426950408889634

V7X_VMEM_LIMIT = 56 * 1024 * 1024
V7X_VMEM_LIMIT_MOE = 60 * 1024 * 1024
MOE_TILE = 256
MOE_CHUNK_ROWS = 128
MOE_RING = 8
SB_TILE = 256
SB_HEADS = 4
SB_ROW_SPLIT = 2
SB_UNDERFLOW_LOG2 = -160.0
ROUTE_TILE = 512
DISPATCH_TILE = 256
COMBINE_TILE = 128
OUT_LANES = 128


def _cparams(sem, vmem=None):
    return pltpu.CompilerParams(dimension_semantics=sem, vmem_limit_bytes=vmem)


def _rms(x, g):
    return x * lax.rsqrt(jnp.mean(x * x, axis=-1, keepdims=True) + NORM_EPS) * g


def _ada_kernel(c_ref, w_ref, b_ref, o_ref):
    c = c_ref[...]
    ca = c / (1.0 + jnp.exp(-c))
    o_ref[...] = jnp.dot(ca.astype(BF16), w_ref[...].astype(BF16),
                         preferred_element_type=F32) + b_ref[...]


def _ada(c_pad, w_ada, b_ada):
    rows, d = c_pad.shape
    n = w_ada.shape[1]
    tn = min(n, 1536)
    return pl.pallas_call(
        _ada_kernel,
        out_shape=jax.ShapeDtypeStruct((rows, n), F32),
        grid=(n // tn,),
        in_specs=[pl.BlockSpec((rows, d), lambda j: (0, 0)),
                  pl.BlockSpec((d, tn), lambda j: (0, j)),
                  pl.BlockSpec((1, tn), lambda j: (0, j))],
        out_specs=pl.BlockSpec((rows, tn), lambda j: (0, j)),
        compiler_params=_cparams(("arbitrary",), V7X_VMEM_LIMIT),
        name="ada",
    )(c_pad, w_ada, b_ada)


def _qkv_kernel(x_ref, mod_ref, g_ref, w_ref, o_ref, h_ref):
    @pl.when(pl.program_id(1) == 0)
    def _():
        h = _rms(x_ref[...], g_ref[...]) * (1.0 + mod_ref[1:2, :]) + mod_ref[0:1, :]
        h_ref[...] = h.astype(BF16)

    o_ref[...] = jnp.dot(h_ref[...], w_ref[...], preferred_element_type=F32).astype(o_ref.dtype)


def _qkv(x2, mod, g, w_bf, seq, col_lo, col_hi, out_dtype):
    n, d = x2.shape
    nout = col_hi - col_lo
    tm = min(1024, seq)
    tn = min(1024, nout)
    j0 = col_lo // tn
    per_b = seq // tm
    return pl.pallas_call(
        _qkv_kernel,
        out_shape=jax.ShapeDtypeStruct((n, nout), out_dtype),
        grid=(n // tm, nout // tn),
        in_specs=[pl.BlockSpec((tm, d), lambda i, j: (i, 0)),
                  pl.BlockSpec((None, 6, d), lambda i, j: (i // per_b, 0, 0)),
                  pl.BlockSpec((1, d), lambda i, j: (0, 0)),
                  pl.BlockSpec((d, tn), lambda i, j: (0, j0 + j))],
        out_specs=pl.BlockSpec((tm, tn), lambda i, j: (i, j)),
        scratch_shapes=[pltpu.VMEM((tm, d), BF16)],
        compiler_params=_cparams(("parallel", "arbitrary"), V7X_VMEM_LIMIT),
        name="qkv",
    )(x2, mod, g, w_bf)


def _dswa_kernel(rb_ref, q_ref, k_ref, v_ref, bucket_ref, g_ref, o_ref, *stats):
    n_pat = len(DSWA_PATTERNS)
    acc_refs, m_refs, l_refs = stats[:n_pat], stats[n_pat:2 * n_pat], stats[2 * n_pat:]
    seq = q_ref.shape[0]
    head = pl.program_id(1)
    scale = 1.0 / math.sqrt(HEAD_DIM)
    blk = DSWA_BLK
    ii = lax.broadcasted_iota(jnp.int32, (blk, 2 * blk), 0)
    jj = lax.broadcasted_iota(jnp.int32, (blk, 2 * blk), 1)
    diff = ii + blk - jj
    band = (diff >= 0) & (diff <= blk)

    for p, (_, dil) in enumerate(DSWA_PATTERNS):
        nb = seq // dil // blk
        bucket = bucket_ref[p]
        bias = jnp.zeros((blk, 2 * blk), F32)
        for b in range(REL_BUCKETS):
            bias = jnp.where(bucket == b, rb_ref[head, b], bias)
        bias = jnp.where(band, bias, NEG_BIG)
        acc_ref, m_ref, l_ref = acc_refs[p], m_refs[p], l_refs[p]

        def rows(start, size, dil=dil):
            if dil == 1:
                return pl.ds(start, size)
            return pl.ds(start, size, stride=dil)

        def group(blocks, bias=bias, rows=rows, dil=dil,
                  acc_ref=acc_ref, m_ref=m_ref, l_ref=l_ref):
            rqs, rks, bss = [], [], []
            for r, n in blocks:
                rq = rows(n * blk * dil + r, blk)
                rqs.append(rq)
                rks.append(rq if n == 0 else rows((n - 1) * blk * dil + r, 2 * blk))
                bss.append(bias[:, blk:] if n == 0 else bias)
            qs = [(q_ref[rq, :] * scale).astype(BF16) for rq in rqs]
            ks = [k_ref[rk, :].astype(BF16) for rk in rks]
            ss = [lax.dot_general(q, k, (((1,), (1,)), ((), ())), preferred_element_type=F32) + bs
                  for q, k, bs in zip(qs, ks, bss)]
            ms = [jnp.max(s, axis=-1, keepdims=True) for s in ss]
            ps = [jnp.exp(s - m) for s, m in zip(ss, ms)]
            vs = [v_ref[rk, :].astype(BF16) for rk in rks]
            for rq, m, pexp, v in zip(rqs, ms, ps, vs):
                acc_ref[rq, :] = jnp.dot(pexp.astype(BF16), v, preferred_element_type=F32)
                m_ref[rq, :] = jnp.broadcast_to(m, (blk, HEAD_DIM))
                l_ref[rq, :] = jnp.broadcast_to(jnp.sum(pexp, axis=-1, keepdims=True),
                                                (blk, HEAD_DIM))

        per_phase = min(nb, DSWA_GROUP)
        phases = max(1, min(dil, DSWA_GROUP // per_phase))

        def trip(t, carry, nb=nb, per_phase=per_phase, phases=phases, group=group):
            for n0 in range(0, nb, per_phase):
                group([(t * phases + dr, n0 + dn) for dr in range(phases) for dn in range(per_phase)])
            return carry

        lax.fori_loop(0, dil // phases, trip, 0)

    chunk = min(256, seq)

    def finish(c, carry):
        rs = pl.ds(pl.multiple_of(c * chunk, chunk), chunk)
        ms = [m_ref[rs, :] for m_ref in m_refs]
        m_all = ms[0]
        for m in ms[1:]:
            m_all = jnp.maximum(m_all, m)
        num = jnp.zeros((chunk, HEAD_DIM), F32)
        den = jnp.zeros((chunk, HEAD_DIM), F32)
        for p in range(n_pat):
            w = jnp.exp(ms[p] - m_all)
            num = num + w * acc_refs[p][rs, :]
            den = den + w * l_refs[p][rs, :]
        o_ref[rs, :] = _rms(num / den, g_ref[...]).astype(o_ref.dtype)
        return carry

    lax.fori_loop(0, seq // chunk, finish, 0)


def _dswa(rel_bias_t, qkv3, buckets, gain, n_heads):
    bsz, seq, _ = qkv3.shape
    w = n_heads * HEAD_DIM
    return pl.pallas_call(
        _dswa_kernel,
        out_shape=jax.ShapeDtypeStruct((bsz, seq, w), BF16),
        grid=(bsz, n_heads),
        in_specs=[pl.BlockSpec(memory_space=pltpu.SMEM),
                  pl.BlockSpec((None, seq, HEAD_DIM), lambda b, h: (b, 0, h)),
                  pl.BlockSpec((None, seq, HEAD_DIM), lambda b, h: (b, 0, n_heads + h)),
                  pl.BlockSpec((None, seq, HEAD_DIM), lambda b, h: (b, 0, 2 * n_heads + h)),
                  pl.BlockSpec(buckets.shape, lambda b, h: (0, 0, 0)),
                  pl.BlockSpec((1, HEAD_DIM), lambda b, h: (0, h))],
        out_specs=pl.BlockSpec((None, seq, HEAD_DIM), lambda b, h: (b, 0, h)),
        scratch_shapes=[pltpu.VMEM((seq, HEAD_DIM), F32)] * (3 * len(DSWA_PATTERNS)),
        compiler_params=_cparams(("parallel", "parallel"), V7X_VMEM_LIMIT),
        name="dswa",
    )(rel_bias_t, qkv3, qkv3, qkv3, buckets, gain)


def _neg_abs(z):
    bits = lax.bitcast_convert_type(z, jnp.uint32) | jnp.uint32(0x80000000)
    return lax.bitcast_convert_type(bits, F32)


def _sb_kernel(q_ref, kb_ref, vb_ref, tri_ref, g_ref, o_ref):
    tq = q_ref.shape[0]
    nh = q_ref.shape[1] // HEAD_DIM
    iq = pl.program_id(2)

    q_all = (q_ref[...] * (LOG2E / math.sqrt(HEAD_DIM))).astype(BF16)
    rs = tq // SB_ROW_SPLIT
    ti = lax.broadcasted_iota(jnp.int32, (tq, tq), 0)
    si = lax.broadcasted_iota(jnp.int32, (tq, tq), 1)
    causal = si < ti
    chains = [(h, s) for h in range(nh) for s in range(SB_ROW_SPLIT)]
    cols = [slice(h * HEAD_DIM, (h + 1) * HEAD_DIM) for h, _ in chains]
    rws = [slice(s * rs, (s + 1) * rs) for _, s in chains]
    qs = [q_all[r, c] for r, c in zip(rws, cols)]
    masks = [causal[r, :] for r in rws]

    def visit(jk, state, masked):
        r0 = pl.multiple_of(jk * tq, tq)
        carries, accs = state
        ks = [kb_ref[pl.ds(r0, tq), c] for c in cols]
        vs = [vb_ref[pl.ds(r0, tq), c] for c in cols]
        zs = [lax.dot_general(q, k, (((1,), (1,)), ((), ())), preferred_element_type=F32)
              for q, k in zip(qs, ks)]
        his, los, logsig, rowsums = [], [], [], []
        for z, m in zip(zs, masks):
            sp = jnp.maximum(z, 0.0) + jnp.log(1.0 + jnp.exp2(_neg_abs(z))) * LOG2E
            if masked:
                sp = jnp.where(m, sp, 0.0)
            hi = sp.astype(BF16)
            his.append(hi)
            los.append((sp - hi.astype(F32)).astype(BF16))
            logsig.append(z - sp)
            rowsums.append(jnp.sum(sp, axis=-1, keepdims=True))
        within = [jnp.dot(jnp.concatenate([hi, lo], axis=1), tri_ref[...], preferred_element_type=F32)
                  for hi, lo in zip(his, los)]
        probs = [jnp.exp2(ls + wi + carry) for ls, wi, carry in zip(logsig, within, carries)]
        if masked:
            probs = [jnp.where(m, a, 0.0) for m, a in zip(masks, probs)]
        accs = tuple(acc + jnp.dot(a.astype(BF16), v, preferred_element_type=F32)
                     for acc, a, v in zip(accs, probs, vs))
        carries = tuple(carry - rsum for carry, rsum in zip(carries, rowsums))
        return carries, accs

    def remaining(carries):
        left = carries[0]
        for c in carries[1:]:
            left = jnp.maximum(left, c)
        return jnp.max(left)

    state = (tuple(jnp.zeros((rs, 1), F32) for _ in chains),
             tuple(jnp.zeros((rs, HEAD_DIM), F32) for _ in chains))
    state = visit(iq, state, True)

    def more(loop):
        t, left, _ = loop
        return (t < iq) & (left > SB_UNDERFLOW_LOG2)

    def step(loop):
        t, _, st = loop
        st = visit(iq - 1 - t, st, False)
        return t + 1, remaining(st[0]), st

    _, _, state = lax.while_loop(more, step, (jnp.int32(0), remaining(state[0]), state))
    for r, c, acc in zip(rws, cols, state[1]):
        o_ref[r, c] = _rms(acc, g_ref[:, c]).astype(o_ref.dtype)


def _sb(q_src, kv, tri2, gain, n_heads, q_col0):
    bsz, seq, _ = kv.shape
    tq = min(SB_TILE, seq)
    nh = min(SB_HEADS, n_heads)
    wblk = nh * HEAD_DIM
    groups = n_heads // nh
    c0 = q_col0 // nh
    return pl.pallas_call(
        _sb_kernel,
        out_shape=jax.ShapeDtypeStruct((bsz, seq, n_heads * HEAD_DIM), BF16),
        grid=(bsz, groups, seq // tq),
        in_specs=[pl.BlockSpec((None, tq, wblk), lambda b, h, i: (b, i, c0 + h)),
                  pl.BlockSpec((None, seq, wblk), lambda b, h, i: (b, 0, h)),
                  pl.BlockSpec((None, seq, wblk), lambda b, h, i: (b, 0, groups + h)),
                  pl.BlockSpec(tri2.shape, lambda b, h, i: (0, 0)),
                  pl.BlockSpec((1, wblk), lambda b, h, i: (0, h))],
        out_specs=pl.BlockSpec((None, tq, wblk), lambda b, h, i: (b, i, h)),
        compiler_params=_cparams(("parallel", "parallel", "arbitrary"), V7X_VMEM_LIMIT),
        name="sb",
    )(q_src, kv, kv, tri2, gain)


def _pack_halves(v):
    c = v.shape[1] // 2
    bits = lax.bitcast_convert_type(v, jnp.uint32)
    return (bits[:, :c] >> 16) | (bits[:, c:] & jnp.uint32(0xFFFF0000))


def _unpack_halves(p):
    lo = lax.bitcast_convert_type(p << 16, F32).astype(BF16)
    hi = lax.bitcast_convert_type(p & jnp.uint32(0xFFFF0000), F32).astype(BF16)
    return jnp.concatenate([lo, hi], axis=1)


def _outproj_kernel(oa_ref, ob_ref, w_ref, x_ref, mod_ref, gpa_ref, gpf_ref, wr_hi_ref, wr_lo_ref,
                    x1_ref, h2p_ref, lg_ref):
    wa = oa_ref.shape[1]
    o = jnp.dot(oa_ref[...], w_ref[0:wa, :], preferred_element_type=F32)
    o = o + jnp.dot(ob_ref[...], w_ref[wa:, :], preferred_element_type=F32)
    x1 = x_ref[...] + mod_ref[2:3, :] * _rms(o, gpa_ref[...])
    x1_ref[...] = x1
    h2 = _rms(x1, gpf_ref[...]) * (1.0 + mod_ref[4:5, :]) + mod_ref[3:4, :]
    hi = h2.astype(BF16)
    hi_f = hi.astype(F32)
    h2p_ref[...] = _pack_halves(hi_f)
    lo = (h2 - hi_f).astype(BF16)
    lg = jnp.dot(hi, wr_hi_ref[...], preferred_element_type=F32)
    lg = lg + jnp.dot(lo, wr_hi_ref[...], preferred_element_type=F32)
    lg = lg + jnp.dot(hi, wr_lo_ref[...], preferred_element_type=F32)
    lg_ref[...] = lg


def _outproj(oa, ob, w_bf, x2, mod, gpa, gpf, wr_hi, wr_lo, seq):
    n, d = x2.shape
    wa = oa.shape[1]
    ne = wr_hi.shape[1]
    tm = min(256, seq)
    per_b = seq // tm
    const = lambda i: (0, 0)
    return pl.pallas_call(
        _outproj_kernel,
        out_shape=(jax.ShapeDtypeStruct((n, d), F32), jax.ShapeDtypeStruct((n, d // 2), jnp.uint32),
                   jax.ShapeDtypeStruct((n, ne), F32)),
        grid=(n // tm,),
        in_specs=[pl.BlockSpec((tm, wa), lambda i: (i, 0)),
                  pl.BlockSpec((tm, wa), lambda i: (i, 0)),
                  pl.BlockSpec(w_bf.shape, const),
                  pl.BlockSpec((tm, d), lambda i: (i, 0)),
                  pl.BlockSpec((None, 6, d), lambda i: (i // per_b, 0, 0)),
                  pl.BlockSpec((1, d), const),
                  pl.BlockSpec((1, d), const),
                  pl.BlockSpec(wr_hi.shape, const),
                  pl.BlockSpec(wr_lo.shape, const)],
        out_specs=(pl.BlockSpec((tm, d), lambda i: (i, 0)),
                   pl.BlockSpec((tm, d // 2), lambda i: (i, 0)),
                   pl.BlockSpec((tm, ne), lambda i: (i, 0))),
        compiler_params=_cparams(("parallel",), V7X_VMEM_LIMIT),
        name="outproj",
    )(oa, ob, w_bf, x2, mod, gpa, gpf, wr_hi, wr_lo)


def _topk(lg, rbias):
    tr, ne = lg.shape
    scores = 1.0 / (1.0 + jnp.exp(-lg))
    sel = scores + rbias
    lane = lax.broadcasted_iota(jnp.int32, (tr, ne), 1).astype(F32)
    onehots, gates = [], []
    chosen = jnp.zeros((tr, ne), F32)
    for _ in range(TOP_K):
        m = jnp.max(sel, axis=-1, keepdims=True)
        idx = jnp.min(jnp.where(sel == m, lane, float(ne)), axis=-1, keepdims=True)
        oh = lane == idx
        gates.append(jnp.sum(jnp.where(oh, scores, 0.0), axis=-1, keepdims=True))
        sel = jnp.where(oh, -jnp.inf, sel)
        chosen = jnp.where(oh, 1.0, chosen)
        onehots.append(oh)
    return onehots, gates, chosen


def _route_count_kernel(lg_ref, rbias_ref, cnt_ref):
    @pl.when(pl.program_id(0) == 0)
    def _():
        cnt_ref[...] = jnp.zeros_like(cnt_ref)

    _, _, chosen = _topk(lg_ref[...], rbias_ref[...])
    cnt_ref[0:1, :] = cnt_ref[0:1, :] + jnp.sum(chosen, axis=0, keepdims=True)


def _route_count(logits, rbias, tr):
    n, ne = logits.shape
    return pl.pallas_call(
        _route_count_kernel,
        out_shape=jax.ShapeDtypeStruct((8, ne), F32),
        grid=(n // tr,),
        in_specs=[pl.BlockSpec((tr, ne), lambda i: (i, 0)),
                  pl.BlockSpec((1, ne), lambda i: (0, 0))],
        out_specs=pl.BlockSpec((8, ne), lambda i: (0, 0)),
        compiler_params=_cparams(("arbitrary",)),
        name="route_count",
    )(logits, rbias)


def _route_assign_kernel(lg_ref, rbias_ref, tri_ref, start_ref, dest_ref, gate_ref, seen_ref):
    tr = lg_ref.shape[0]

    @pl.when(pl.program_id(0) == 0)
    def _():
        seen_ref[...] = jnp.zeros_like(seen_ref)

    onehots, gates, chosen = _topk(lg_ref[...], rbias_ref[...])
    gsum = gates[0]
    for g in gates[1:]:
        gsum = gsum + g
    before = jnp.dot(tri_ref[...], chosen.astype(BF16), preferred_element_type=F32)
    slot = before + seen_ref[0:1, :] + start_ref[...]
    seen_ref[0:1, :] = seen_ref[0:1, :] + jnp.sum(chosen, axis=0, keepdims=True)

    out_lane = lax.broadcasted_iota(jnp.int32, (tr, OUT_LANES), 1)
    dest_o = jnp.zeros((tr, OUT_LANES), jnp.int32)
    gate_o = jnp.zeros((tr, OUT_LANES), F32)
    for k in range(TOP_K):
        dest = jnp.sum(jnp.where(onehots[k], slot, 0.0), axis=-1, keepdims=True)
        dest_o = jnp.where(out_lane == k, dest.astype(jnp.int32), dest_o)
        gate_o = jnp.where(out_lane == k, gates[k] / gsum * ROUTED_SCALING, gate_o)
    dest_ref[...] = dest_o
    gate_ref[...] = gate_o


def _route_assign(logits, rbias, tri, start):
    n, ne = logits.shape
    tr = tri.shape[0]
    return pl.pallas_call(
        _route_assign_kernel,
        out_shape=(jax.ShapeDtypeStruct((n, OUT_LANES), jnp.int32),
                   jax.ShapeDtypeStruct((n, OUT_LANES), F32)),
        grid=(n // tr,),
        in_specs=[pl.BlockSpec((tr, ne), lambda i: (i, 0)),
                  pl.BlockSpec((1, ne), lambda i: (0, 0)),
                  pl.BlockSpec((tr, tr), lambda i: (0, 0)),
                  pl.BlockSpec((1, ne), lambda i: (0, 0))],
        out_specs=(pl.BlockSpec((tr, OUT_LANES), lambda i: (i, 0)),
                   pl.BlockSpec((tr, OUT_LANES), lambda i: (i, 0))),
        scratch_shapes=[pltpu.VMEM((8, ne), F32)],
        compiler_params=_cparams(("arbitrary",)),
        name="route_assign",
    )(logits, rbias, tri, start)


def _dispatch_shared_kernel(dest_ref, lo_ref, hi_ref, h_ref, wg_ref, wu_ref, wd_ref,
                            y_ref, xs_ref, zero_ref, sem, zsem):
    td = h_ref.shape[0]
    step = pl.program_id(0)
    n_exp = lo_ref.shape[0]
    tile = zero_ref.shape[0]

    @pl.when(step == 0)
    def _():
        zero_ref[...] = jnp.zeros_like(zero_ref)

        def zcopy(p):
            return pltpu.make_async_copy(zero_ref.at[pl.ds(0, 1)], xs_ref.at[pl.ds(p, 1)], zsem.at[0])

        def fill(e, c):
            lax.fori_loop(lo_ref[e], hi_ref[e], lambda p, c2: (zcopy(p).start(), c2)[1], 0)
            lax.fori_loop(lo_ref[e], hi_ref[e], lambda p, c2: (zcopy(p).wait(), c2)[1], 0)
            return c

        lax.fori_loop(0, n_exp, fill, 0)

        def tcopy(t):
            rows = pl.ds(pl.multiple_of(t * tile, tile), tile)
            return pltpu.make_async_copy(zero_ref, xs_ref.at[rows], zsem.at[1])

        first, last = hi_ref[n_exp - 1] // tile, xs_ref.shape[0] // tile
        lax.fori_loop(first, last, lambda t, c2: (tcopy(t).start(), c2)[1], 0)
        lax.fori_loop(first, last, lambda t, c2: (tcopy(t).wait(), c2)[1], 0)

    base = step * (td * TOP_K)

    def copy(t, k):
        return pltpu.make_async_copy(h_ref.at[pl.ds(t, 1)],
                                     xs_ref.at[pl.ds(dest_ref[base + t * TOP_K + k], 1)], sem.at[0])

    for t in range(td):
        for k in range(TOP_K):
            copy(t, k).start()

    x = _unpack_halves(h_ref[...])
    hg = jnp.dot(x, wg_ref[...], preferred_element_type=F32)
    hu = jnp.dot(x, wu_ref[...], preferred_element_type=F32)
    h = (hg / (1.0 + jnp.exp(-hg))) * hu
    y_ref[...] = jnp.dot(h.astype(BF16), wd_ref[...], preferred_element_type=F32)

    for t in range(td):
        for k in range(TOP_K):
            copy(t, k).wait()


def _dispatch_shared(dest_flat, fill_lo, fill_hi, h2p, n_rows, wg, wu, wd):
    n, dp = h2p.shape
    d, f = wg.shape
    td = min(DISPATCH_TILE, n)
    const = lambda i, *_: (0, 0)
    return pl.pallas_call(
        _dispatch_shared_kernel,
        out_shape=(jax.ShapeDtypeStruct((n, d), F32), jax.ShapeDtypeStruct((n_rows, dp), h2p.dtype)),
        grid_spec=pltpu.PrefetchScalarGridSpec(
            num_scalar_prefetch=3,
            grid=(n // td,),
            in_specs=[pl.BlockSpec((td, dp), lambda i, *_: (i, 0)),
                      pl.BlockSpec((d, f), const),
                      pl.BlockSpec((d, f), const),
                      pl.BlockSpec((f, d), const)],
            out_specs=(pl.BlockSpec((td, d), lambda i, *_: (i, 0)),
                       pl.BlockSpec(memory_space=pl.ANY)),
            scratch_shapes=[pltpu.VMEM((MOE_TILE, dp), h2p.dtype),
                            pltpu.SemaphoreType.DMA((1,)),
                            pltpu.SemaphoreType.DMA((2,))]),
        compiler_params=_cparams(("arbitrary",), V7X_VMEM_LIMIT),
        name="dispatch_shared",
    )(dest_flat, fill_lo, fill_hi, h2p, wg, wu, wd)


def _moe_kernel(te_ref, nu_ref, slot_ref, nxt_ref, ulo_ref, uhi_ref,
                x_ref, wg_hbm, wu_hbm, wd_hbm, y_ref,
                wg_bf, wu_bf, wd_bf, stage_gu, stage_d, sem):
    i = pl.program_id(0)
    d, f = wg_bf.shape[1], wg_bf.shape[2]
    ring, rg, rd = stage_gu.shape[0], stage_gu.shape[1], stage_d.shape[1]
    ahead = ring - 1
    ng, nd = d // rg, f // rd
    n_units = 2 * ng + nd
    kinds = ((0, ng, wg_hbm, wg_bf, stage_gu, rg), (ng, 2 * ng, wu_hbm, wu_bf, stage_gu, rg),
             (2 * ng, n_units, wd_hbm, wd_bf, stage_d, rd))

    def chunk_copy(kind, u, e):
        first, _, src, _, stage, rows = kind
        r0 = pl.multiple_of((u - first) * rows, rows)
        return pltpu.make_async_copy(src.at[e, pl.ds(r0, rows), :], stage.at[u % ring],
                                     sem.at[u % ring])

    def request(u, e):
        for kind in kinds:
            @pl.when((u >= kind[0]) & (u < kind[1]))
            def _(kind=kind):
                chunk_copy(kind, u, e).start()

    def receive(u, e, slot):
        for kind in kinds:
            @pl.when((u >= kind[0]) & (u < kind[1]))
            def _(kind=kind):
                first, _, _, dst, stage, rows = kind
                chunk_copy(kind, u, e).wait()
                r0 = pl.multiple_of((u - first) * rows, rows)
                dst[slot, pl.ds(r0, rows), :] = stage[u % ring].astype(BF16)

    def run_units(lo, hi, e, slot):
        def body(u, c):
            @pl.when(u == 0)
            def _():
                for j in range(ahead):
                    chunk_copy(kinds[0], j, e).start()

            @pl.when(u + ahead < n_units)
            def _():
                request(u + ahead, e)

            receive(u, e, slot)
            return c

        lax.fori_loop(lo, hi, body, 0)

    slot = slot_ref[i]

    @pl.when(i == 0)
    def _():
        run_units(0, n_units, te_ref[0], slot)

    run_units(ulo_ref[i], uhi_ref[i], nxt_ref[i], 1 - slot)

    @pl.when(i < nu_ref[0])
    def _():
        x = _unpack_halves(x_ref[...])
        hg = jnp.dot(x, wg_bf[slot], preferred_element_type=F32)
        hu = jnp.dot(x, wu_bf[slot], preferred_element_type=F32)
        h = (hg / (1.0 + jnp.exp(-hg))) * hu
        y_ref[...] = jnp.dot(h.astype(BF16), wd_bf[slot], preferred_element_type=F32)

    @pl.when(i >= nu_ref[0])
    def _():
        y_ref[...] = jnp.zeros_like(y_ref)


def _moe(tile_expert, n_used, tile_slot, tile_next, unit_lo, unit_hi, xs, wg, wu, wd):
    rows, dp = xs.shape
    _, d, f = wg.shape
    tm = MOE_TILE
    rg = rd = MOE_CHUNK_ROWS
    ring = min(MOE_RING, d // rg + 1)
    return pl.pallas_call(
        _moe_kernel,
        out_shape=jax.ShapeDtypeStruct((rows, d), F32),
        grid_spec=pltpu.PrefetchScalarGridSpec(
            num_scalar_prefetch=6,
            grid=(rows // tm,),
            in_specs=[pl.BlockSpec((tm, dp), lambda i, te, nu, *_: (jnp.minimum(i, nu[0] - 1), 0)),
                      pl.BlockSpec(memory_space=pl.ANY),
                      pl.BlockSpec(memory_space=pl.ANY),
                      pl.BlockSpec(memory_space=pl.ANY)],
            out_specs=pl.BlockSpec((tm, d), lambda i, *_: (i, 0)),
            scratch_shapes=[pltpu.VMEM((2, d, f), BF16),
                            pltpu.VMEM((2, d, f), BF16),
                            pltpu.VMEM((2, f, d), BF16),
                            pltpu.VMEM((ring, rg, f), F32),
                            pltpu.VMEM((ring, rd, d), F32),
                            pltpu.SemaphoreType.DMA((ring,))]),
        compiler_params=_cparams(("arbitrary",), V7X_VMEM_LIMIT_MOE),
        name="moe",
    )(tile_expert, n_used, tile_slot, tile_next, unit_lo, unit_hi, xs, wg, wu, wd)


def _combine_kernel(dest_ref, y_ref, gate_ref, sh_ref, x1_ref, mod_ref, g_ref, o_ref, buf_ref, sem):
    tc = x1_ref.shape[0]
    step = pl.program_id(0)
    n_steps = pl.num_programs(0)

    def copy(tile, slot, t, k):
        src = dest_ref[(tile * tc + t) * TOP_K + k]
        return pltpu.make_async_copy(y_ref.at[pl.ds(src, 1)], buf_ref.at[slot, k, pl.ds(t, 1)],
                                     sem.at[slot])

    def request(tile, slot):
        def body(t, c):
            for k in range(TOP_K):
                copy(tile, slot, t, k).start()
            return c
        lax.fori_loop(0, tc, body, 0)

    slot = step % 2

    @pl.when(step == 0)
    def _():
        request(step, slot)

    @pl.when(step + 1 < n_steps)
    def _():
        request(step + 1, 1 - slot)

    def await_(t, c):
        for k in range(TOP_K):
            copy(step, slot, t, k).wait()
        return c

    lax.fori_loop(0, tc, await_, 0)

    gate = gate_ref[...]
    y = sh_ref[...]
    for k in range(TOP_K):
        y = y + gate[:, k:k + 1] * buf_ref[slot, k]
    o_ref[...] = x1_ref[...] + mod_ref[5:6, :] * _rms(y, g_ref[...])


def _combine(dest_flat, y, gates, shared, x1, mod, g, seq):
    n, d = x1.shape
    tc = min(COMBINE_TILE, seq)
    per_b = seq // tc
    return pl.pallas_call(
        _combine_kernel,
        out_shape=jax.ShapeDtypeStruct((n, d), F32),
        grid_spec=pltpu.PrefetchScalarGridSpec(
            num_scalar_prefetch=1,
            grid=(n // tc,),
            in_specs=[pl.BlockSpec(memory_space=pl.ANY),
                      pl.BlockSpec((tc, OUT_LANES), lambda i, dest: (i, 0)),
                      pl.BlockSpec((tc, d), lambda i, dest: (i, 0)),
                      pl.BlockSpec((tc, d), lambda i, dest: (i, 0)),
                      pl.BlockSpec((None, 6, d), lambda i, dest: (i // per_b, 0, 0)),
                      pl.BlockSpec((1, d), lambda i, dest: (0, 0))],
            out_specs=pl.BlockSpec((tc, d), lambda i, dest: (i, 0)),
            scratch_shapes=[pltpu.VMEM((2, TOP_K, tc, d), F32),
                            pltpu.SemaphoreType.DMA((2,))]),
        compiler_params=_cparams(("arbitrary",), V7X_VMEM_LIMIT),
        name="combine",
    )(dest_flat, y, gates, shared, x1, mod, g)


def _t5_bucket(dist):
    max_exact = REL_BUCKETS // 2
    d_f = jnp.maximum(dist, max_exact).astype(F32)
    large = max_exact + (jnp.log(d_f / max_exact) / math.log(REL_MAX_DISTANCE / max_exact)
                         * (REL_BUCKETS - max_exact)).astype(jnp.int32)
    large = jnp.minimum(large, REL_BUCKETS - 1)
    return jnp.where(dist < max_exact, dist, large)


def _bucket_tables():
    i = jnp.arange(DSWA_BLK)[:, None]
    j = jnp.arange(2 * DSWA_BLK)[None, :]
    diff = jnp.maximum(i + DSWA_BLK - j, 0)
    return jnp.stack([_t5_bucket(diff * dil) for _, dil in DSWA_PATTERNS]).astype(jnp.int32)


def _strict_lower(n):
    return jnp.arange(n)[:, None] > jnp.arange(n)[None, :]


def _layer(x, c_pad, rel_bias, norm_pre_attn, norm_post_attn, norm_pre_ffn, norm_post_ffn,
           w_ada, b_ada, w_in, head_gain_dswa, head_gain_sb, w_out, w_router, router_bias,
           w_gate_e, w_up_e, w_down_e, w_gate_s, w_up_s, w_down_s):
    bsz, seq, d = x.shape
    n = bsz * seq
    n_heads = d // (2 * HEAD_DIM)
    n_exp = w_router.shape[1]
    row = lambda v: v.reshape(1, -1)

    mod = _ada(c_pad, w_ada, row(b_ada))[:bsz].reshape(bsz, 6, d)
    x2 = x.reshape(n, d)

    w_in_bf = w_in.astype(BF16)
    qkvq = _qkv(x2, mod, row(norm_pre_attn), w_in_bf, seq, 0, 2 * d, F32).reshape(bsz, seq, 2 * d)
    kv_sb = _qkv(x2, mod, row(norm_pre_attn), w_in_bf, seq, 2 * d, 3 * d, BF16).reshape(bsz, seq, d)

    oa = _dswa(rel_bias.T, qkvq, _bucket_tables(), row(head_gain_dswa), n_heads)
    tq = min(SB_TILE, seq)
    neg_tri = jnp.where(_strict_lower(tq), -1.0, 0.0).astype(BF16)
    ob = _sb(qkvq, kv_sb, jnp.concatenate([neg_tri, neg_tri], axis=0), row(head_gain_sb), n_heads,
             3 * n_heads)

    wr_hi = w_router.astype(BF16)
    wr_lo = (w_router - wr_hi.astype(F32)).astype(BF16)
    x1, h2p, logits = _outproj(oa.reshape(n, -1), ob.reshape(n, -1), w_out.astype(BF16), x2, mod,
                                   row(norm_post_attn), row(norm_pre_ffn), wr_hi, wr_lo, seq)

    tr = min(ROUTE_TILE, n)
    counts = _route_count(logits, row(router_bias), tr)[0].astype(jnp.int32)
    padded = (counts + MOE_TILE - 1) // MOE_TILE * MOE_TILE
    pad_end = jnp.cumsum(padded)
    pad_start = pad_end - padded
    n_rows = (n * TOP_K + n_exp * (MOE_TILE - 1) + MOE_TILE - 1) // MOE_TILE * MOE_TILE
    n_tiles = n_rows // MOE_TILE
    n_used = jnp.maximum(pad_end[-1] // MOE_TILE, 1).astype(jnp.int32)
    tile_id = jnp.minimum(jnp.arange(n_tiles, dtype=jnp.int32), n_used - 1)
    tile_expert = jnp.minimum(
        jnp.sum(pad_end[None, :] <= (tile_id * MOE_TILE)[:, None], axis=1), n_exp - 1).astype(jnp.int32)

    n_units = (2 * d + w_gate_e.shape[2]) // MOE_CHUNK_ROWS
    nonempty = padded > 0
    ids = jnp.arange(n_exp, dtype=jnp.int32)
    later = lax.cummin(jnp.where(nonempty, ids, n_exp)[::-1])[::-1]
    next_e = jnp.concatenate([later[1:], jnp.full((1,), n_exp, jnp.int32)])
    order = jnp.cumsum(nonempty.astype(jnp.int32)) - 1
    mine = tile_expert[:, None] == ids[None, :]
    per_tile = lambda v: jnp.sum(jnp.where(mine, v[None, :], 0), axis=1)
    t_in = tile_id - per_tile(pad_start // MOE_TILE)
    t_cnt = jnp.maximum(per_tile(padded // MOE_TILE), 1)
    nxt = per_tile(next_e)
    live = (jnp.arange(n_tiles) < n_used) & (nxt < n_exp)
    unit_lo = jnp.where(live, n_units * t_in // t_cnt, 0).astype(jnp.int32)
    unit_hi = jnp.where(live, n_units * (t_in + 1) // t_cnt, 0).astype(jnp.int32)
    tile_next = jnp.minimum(nxt, n_exp - 1).astype(jnp.int32)
    tile_slot = (per_tile(order) % 2).astype(jnp.int32)

    dest, gates = _route_assign(logits, row(router_bias), _strict_lower(tr).astype(BF16),
                                row(pad_start.astype(F32)))
    dest_flat = dest[:, :TOP_K].reshape(-1)

    shared, xs = _dispatch_shared(dest_flat, (pad_start + counts).astype(jnp.int32),
                                  pad_end.astype(jnp.int32), h2p, n_rows, w_gate_s.astype(BF16),
                                  w_up_s.astype(BF16), w_down_s.astype(BF16))
    y = _moe(tile_expert, n_used.reshape(1), tile_slot, tile_next, unit_lo, unit_hi, xs,
             w_gate_e, w_up_e, w_down_e)
    out = _combine(dest_flat, y, gates, shared, x1, mod, row(norm_post_ffn), seq)
    return out.reshape(bsz, seq, d)


def kernel(x, c, rel_bias, norm_pre_attn, norm_post_attn, norm_pre_ffn, norm_post_ffn, w_ada, b_ada, w_in, head_gain_dswa, head_gain_sb, w_out, w_router, router_bias, w_gate_e, w_up_e, w_down_e, w_gate_s, w_up_s, w_down_s):
    bsz = x.shape[0]
    depth = w_ada.shape[0]
    c_pad = jnp.zeros((8, c.shape[1]), F32).at[:bsz].set(c)
    for l in range(depth):
        x = _layer(x, c_pad, rel_bias, norm_pre_attn[l], norm_post_attn[l], norm_pre_ffn[l],
                   norm_post_ffn[l], w_ada[l], b_ada[l], w_in[l], head_gain_dswa[l], head_gain_sb[l],
                   w_out[l], w_router[l], router_bias[l], w_gate_e[l], w_up_e[l], w_down_e[l],
                   w_gate_s[l], w_up_s[l], w_down_s[l])
    return x
```
